```python
import math
import jax
import jax.numpy as jnp
from jax import lax
import numpy as np

D_MODEL = 1024
BATCH = 2
SEQ = 8192
DEPTH = 2
DEC_BATCH = 128
DEC_SEQ = 8
PAST_LEN = 16384
PAGE_SIZE = 128

HEAD_DIM = 64
FOX_HEADS = 4
MLA_HEADS = 4
MLA_NOPE = 64
MLA_ROPE = 32
MLA_V = 64
MLA_KV_RANK = 128
MLA_Q_RANK = 192
ROPE_BASE = 10000.0
DIFF_HEADS = 4
DIFF_HALF = HEAD_DIM // 2
DSA_HEADS = 4
IDX_HEADS = 8
IDX_DIM = 32
IDX_TOPK = 256
N_BUCKETS = 32
MAX_DISTANCE = 128
N_EXPERTS = 32
TOP_K = 4
D_FF = D_MODEL
SWIGLU_LIMIT = 7.0
SWIGLU_ALPHA = 1.702
EXPERT_BLOCK = 128
Q_BLOCK = 128
DEEPNORM_ALPHA = (2 * DEPTH) ** 0.25
DEEPNORM_BETA = (8 * DEPTH) ** -0.25
NEG_INF = -1e30
EPS = 1e-5

IN_SIZES = (FOX_HEADS * HEAD_DIM, HEAD_DIM, HEAD_DIM, FOX_HEADS,
            MLA_Q_RANK, MLA_KV_RANK, MLA_ROPE,
            DIFF_HEADS * HEAD_DIM, HEAD_DIM, HEAD_DIM,
            DSA_HEADS * HEAD_DIM, HEAD_DIM, HEAD_DIM, IDX_HEADS * IDX_DIM, IDX_DIM, IDX_HEADS)
IN_WIDTH = sum(IN_SIZES)
MIX_WIDTH = (FOX_HEADS + DIFF_HEADS + DSA_HEADS) * HEAD_DIM + MLA_HEADS * MLA_V
CACHE_NAMES = ('fox_kv', 'fox_logf', 'mla_ckv', 'mla_krope', 'diff_kv', 'dsa_kv', 'dsa_kidx')

kernel_name = 'hybrid_parallel_heads_decoder_step'


def rms_norm(x, g):
    xf = x.astype(jnp.float32)
    y = xf * lax.rsqrt(jnp.mean(xf * xf, axis=-1, keepdims=True) + EPS)
    return y.astype(x.dtype) * g


def layer_norm(x, g, b):
    xf = x.astype(jnp.float32)
    mu = jnp.mean(xf, axis=-1, keepdims=True)
    var = jnp.mean(jnp.square(xf - mu), axis=-1, keepdims=True)
    return ((xf - mu) * lax.rsqrt(var + EPS)).astype(x.dtype) * g + b


def masked_softmax(s, mask):
    return jax.nn.softmax(jnp.where(mask, s, NEG_INF), axis=-1)


def rope(x, pos):
    half = x.shape[-1] // 2
    inv_freq = ROPE_BASE ** (-jnp.arange(half, dtype=jnp.float32) / half)
    ang = pos.astype(jnp.float32)[:, None] * inv_freq[None, :]
    shape = (pos.shape[0],) + (1,) * (x.ndim - 3) + (half,)
    cos = jnp.cos(ang).reshape(shape).astype(x.dtype)
    sin = jnp.sin(ang).reshape(shape).astype(x.dtype)
    x1, x2 = x[..., :half], x[..., half:]
    return jnp.concatenate([x1 * cos - x2 * sin, x1 * sin + x2 * cos], axis=-1)


def t5_bucket(rel):
    n = jnp.maximum(rel, 0)
    max_exact = N_BUCKETS // 2
    nf = jnp.maximum(n, 1).astype(jnp.float32)
    large = max_exact + (jnp.log(nf / max_exact) / math.log(MAX_DISTANCE / max_exact)
                         * (N_BUCKETS - max_exact)).astype(jnp.int32)
    return jnp.where(n < max_exact, n, jnp.minimum(large, N_BUCKETS - 1))


def gather_pages(cache, layer, page_table):
    rows = cache[layer, page_table]
    return rows.reshape((page_table.shape[0], page_table.shape[1] * cache.shape[2]) + cache.shape[3:])


def project_heads(h, pos, prm):
    B, T, _ = h.shape
    split_points = [int(v) for v in np.cumsum(IN_SIZES)[:-1]]
    (fq, fk, fv, ff, cq, ckv, kr, dq, dk, dv, sq, sk, sv, iq, ik, iw) = jnp.split(
        h @ prm['w_in'], split_points, axis=-1)
    logf = jax.nn.log_sigmoid((ff + prm['fox_f_bias']).astype(jnp.float32)).astype(h.dtype)
    q_mla = (rms_norm(cq, prm['mla_q_norm']) @ prm['mla_w_uq']).reshape(B, T, MLA_HEADS, MLA_NOPE + MLA_ROPE)
    q_lat = jnp.einsum('bthd,rhd->bthr', q_mla[..., :MLA_NOPE], prm['mla_w_uk'])
    rows = {
        'fox_kv': jnp.stack([fk, fv], axis=-2),
        'fox_logf': logf,
        'mla_ckv': rms_norm(ckv, prm['mla_kv_norm']),
        'mla_krope': rope(kr, pos),
        'diff_kv': jnp.stack([dk, dv], axis=-2),
        'dsa_kv': jnp.stack([sk, sv], axis=-2),
        'dsa_kidx': ik,
    }
    queries = {
        'fox_q': fq.reshape(B, T, FOX_HEADS, HEAD_DIM),
        'mla_q_lat': q_lat,
        'mla_q_rope': rope(q_mla[..., MLA_NOPE:], pos),
        'diff_q': dq.reshape(B, T, DIFF_HEADS, HEAD_DIM),
        'dsa_q': sq.reshape(B, T, DSA_HEADS, HEAD_DIM),
        'idx_q': iq.reshape(B, T, IDX_HEADS, IDX_DIM),
        'idx_w': iw,
    }
    return rows, queries


def attend_groups(q, qpos, kv, kpos, prm, lam, lam_init, t5_table):
    f32 = jnp.float32
    B, Tq = q['fox_q'].shape[:2]
    Tk = kpos.shape[0]
    mask = kpos[None, :] <= qpos[:, None]

    fk, fv = kv['fox_kv'][..., 0, :], kv['fox_kv'][..., 1, :]
    s = jnp.einsum('bqhd,bkd->bhqk', q['fox_q'], fk).astype(f32) * HEAD_DIM ** -0.5
    s = s + jnp.swapaxes(q['fox_F'], 1, 2)[..., :, None] - jnp.swapaxes(kv['fox_F'], 1, 2)[..., None, :]
    p = masked_softmax(s, mask).astype(fv.dtype)
    o_fox = jnp.einsum('bhqk,bkd->bqhd', p, fv)

    ckv, kr = kv['mla_ckv'], kv['mla_krope']
    s = (jnp.einsum('bqhr,bkr->bhqk', q['mla_q_lat'], ckv)
         + jnp.einsum('bqhd,bkd->bhqk', q['mla_q_rope'], kr)).astype(f32) * (MLA_NOPE + MLA_ROPE) ** -0.5
    p = masked_softmax(s, mask).astype(ckv.dtype)
    o_mla = jnp.einsum('bqhr,rhd->bqhd', jnp.einsum('bhqk,bkr->bqhr', p, ckv), prm['mla_w_uv'])

    dk, dv = kv['diff_kv'][..., 0, :], kv['diff_kv'][..., 1, :]
    dq = q['diff_q']
    t5_dense = jnp.moveaxis(t5_table[:, :DIFF_HEADS][t5_bucket(qpos[:, None] - kpos[None, :])], -1, 0).astype(f32)
    s1 = jnp.einsum('bqhd,bkd->bhqk', dq[..., :DIFF_HALF], dk[..., :DIFF_HALF]).astype(f32) * DIFF_HALF ** -0.5 + t5_dense
    s2 = jnp.einsum('bqhd,bkd->bhqk', dq[..., DIFF_HALF:], dk[..., DIFF_HALF:]).astype(f32) * DIFF_HALF ** -0.5 + t5_dense
    p = (masked_softmax(s1, mask) - lam * masked_softmax(s2, mask)).astype(dv.dtype)
    o_diff = rms_norm(jnp.einsum('bhqk,bkd->bqhd', p, dv), prm['diff_subln']) * (1.0 - lam_init)

    sk, sv = kv['dsa_kv'][..., 0, :], kv['dsa_kv'][..., 1, :]
    rel = jax.nn.relu(jnp.einsum('bqhd,bkd->bqhk', q['idx_q'], kv['dsa_kidx']).astype(f32) * IDX_DIM ** -0.5)
    score = jnp.einsum('bqh,bqhk->bqk', q['idx_w'].astype(f32) * IDX_HEADS ** -0.5, rel)
    score = jnp.where(mask[None], score, NEG_INF)
    n_sel = min(IDX_TOPK, Tk // 4)
    _, sel = lax.top_k(score, n_sel)
    take = jax.vmap(lambda rows, idx: rows[idx])
    k_sel, v_sel = take(sk, sel), take(sv, sel)
    kpos_sel = kpos[sel]
    valid = kpos_sel <= qpos[None, :, None]
    bias = jnp.moveaxis(t5_table[:, DIFF_HEADS:][t5_bucket(qpos[None, :, None] - kpos_sel)], -1, 1).astype(f32)
    s = jnp.einsum('bqhd,bqkd->bhqk', q['dsa_q'], k_sel).astype(f32) * HEAD_DIM ** -0.5 + bias
    p = masked_softmax(s, valid[:, None]).astype(v_sel.dtype)
    o_dsa = jnp.einsum('bhqk,bqkd->bqhd', p, v_sel)

    return jnp.concatenate([o.reshape(B, Tq, -1) for o in (o_fox, o_mla, o_diff, o_dsa)], axis=-1)


def routed_ffn(h, prm):
    T, D = h.shape
    logits = (h @ prm['router_w'] + prm['router_b']).astype(jnp.float32)
    top_val, top_idx = lax.top_k(logits, TOP_K)
    gates = jax.nn.softmax(top_val, axis=-1).astype(h.dtype)
    A = T * TOP_K
    e_flat = top_idx.reshape(A)
    order = jnp.argsort(e_flat)
    e_sorted = e_flat[order]
    tok_sorted = order // TOP_K
    gate_sorted = gates.reshape(A)[order]
    counts = jnp.zeros((N_EXPERTS,), jnp.int32).at[e_flat].add(1)
    padded = (counts + EXPERT_BLOCK - 1) // EXPERT_BLOCK * EXPERT_BLOCK
    start = jnp.cumsum(counts) - counts
    pstart = jnp.cumsum(padded) - padded
    dest = pstart[e_sorted] + (jnp.arange(A, dtype=jnp.int32) - start[e_sorted])
    n_blocks = -(-A // EXPERT_BLOCK) + N_EXPERTS
    src = jnp.full((n_blocks * EXPERT_BLOCK,), T, jnp.int32).at[dest].set(tok_sorted)
    h_pad = jnp.concatenate([h, jnp.zeros((1, D), h.dtype)], axis=0)[src].reshape(n_blocks, EXPERT_BLOCK, D)
    blk_expert = jnp.minimum(
        jnp.searchsorted(pstart + padded, jnp.arange(n_blocks, dtype=jnp.int32) * EXPERT_BLOCK, side='right'),
        N_EXPERTS - 1)

    def expert_block(args):
        xb, e = args
        gu = xb @ prm['exp_w_gu'][e] + prm['exp_b_gu'][e]
        g = jnp.minimum(gu[:, :D_FF], SWIGLU_LIMIT)
        u = jnp.clip(gu[:, D_FF:], -SWIGLU_LIMIT, SWIGLU_LIMIT)
        act = g * jax.nn.sigmoid(SWIGLU_ALPHA * g) * (u + 1.0)
        return act @ prm['exp_w_dn'][e] + prm['exp_b_dn'][e]

    y_pad = lax.map(expert_block, (h_pad, blk_expert)).reshape(n_blocks * EXPERT_BLOCK, D)
    return jax.ops.segment_sum(y_pad[dest] * gate_sorted[:, None], tok_sorted, num_segments=T)


def trunk_layer(layer, x, c, qpos, kpos, past, prm, t5_table):
    B, T, D = x.shape
    mod = (jax.nn.silu(c) @ prm['w_ada'] + prm['b_ada'])[:, None, :]
    shift1, scale1, gate1, shift2, scale2, gate2 = jnp.split(mod, 6, axis=-1)
    h = x * (1.0 + scale1) + shift1
    rows, q = project_heads(h, qpos, prm)
    if past is None:
        kv = dict(rows)
    else:
        kv = {n: jnp.concatenate([past[n], rows[n]], axis=1) for n in CACHE_NAMES}
    kv['fox_F'] = jnp.cumsum(kv['fox_logf'].astype(jnp.float32), axis=1)
    q['fox_F'] = kv['fox_F'][:, -T:]
    lam_init = 0.8 - 0.6 * math.exp(-0.3 * layer)
    dl = prm['diff_lambda']
    lam = (jnp.exp(jnp.sum(dl[0] * dl[1]).astype(jnp.float32))
           - jnp.exp(jnp.sum(dl[2] * dl[3]).astype(jnp.float32)) + lam_init)

    def attend(q_blk, qpos_blk):
        return attend_groups(q_blk, qpos_blk, kv, kpos, prm, lam, lam_init, t5_table)

    if past is None:
        nb = T // Q_BLOCK
        to_blocks = lambda a: jnp.moveaxis(a.reshape((B, nb, Q_BLOCK) + a.shape[2:]), 1, 0)
        o = lax.map(lambda args: attend(args[0], args[1]),
                    (jax.tree_util.tree_map(to_blocks, q), qpos.reshape(nb, Q_BLOCK)))
        o = jnp.moveaxis(o, 0, 1).reshape(B, T, MIX_WIDTH)
    else:
        o = attend(q, qpos)
    x = layer_norm(DEEPNORM_ALPHA * x + gate1 * (o @ prm['w_out']), prm['ln_g'][0], prm['ln_b'][0])
    h = x * (1.0 + scale2) + shift2
    y = routed_ffn(h.reshape(B * T, D), prm).reshape(B, T, D)
    x = layer_norm(DEEPNORM_ALPHA * x + gate2 * y, prm['ln_g'][1], prm['ln_b'][1])
    return x, rows


def setup_inputs(seed: int = 0) -> dict:
    key = jax.random.key(seed)
    ks = jax.random.split(key, 33)
    n_pages = PAST_LEN // PAGE_SIZE
    n_pool = (DEC_BATCH * n_pages * 5) // 4
    pool = (DEPTH, n_pool, PAGE_SIZE)

    def nrm(i, shape, scale=1.0):
        return jax.random.normal(ks[i], shape, jnp.float32) * scale

    page_table = jax.random.permutation(ks[11], n_pool)[:DEC_BATCH * n_pages].reshape(
        DEC_BATCH, n_pages).astype(jnp.int32)
    return {
        'x_prompt': nrm(0, (BATCH, SEQ, D_MODEL)),
        'x_sample': nrm(1, (DEC_BATCH, DEC_SEQ, D_MODEL)),
        'c_prompt': nrm(2, (BATCH, D_MODEL)),
        'c_sample': nrm(3, (DEC_BATCH, D_MODEL)),
        'cache_fox_kv': nrm(4, pool + (2, HEAD_DIM)),
        'cache_fox_logf': jax.nn.log_sigmoid(4.0 + nrm(5, pool + (FOX_HEADS,))),
        'cache_mla_ckv': nrm(6, pool + (MLA_KV_RANK,)),
        'cache_mla_krope': nrm(7, pool + (MLA_ROPE,)),
        'cache_diff_kv': nrm(8, pool + (2, HEAD_DIM)),
        'cache_dsa_kv': nrm(9, pool + (2, HEAD_DIM)),
        'cache_dsa_kidx': nrm(10, pool + (IDX_DIM,)),
        'page_table': page_table,
        'w_in': nrm(12, (DEPTH, D_MODEL, IN_WIDTH), D_MODEL ** -0.5),
        'fox_f_bias': 4.0 + nrm(13, (DEPTH, FOX_HEADS), 0.1),
        'mla_q_norm': 1.0 + nrm(14, (DEPTH, MLA_Q_RANK), 0.02),
        'mla_w_uq': nrm(15, (DEPTH, MLA_Q_RANK, MLA_HEADS * (MLA_NOPE + MLA_ROPE)), MLA_Q_RANK ** -0.5),
        'mla_kv_norm': 1.0 + nrm(16, (DEPTH, MLA_KV_RANK), 0.02),
        'mla_w_uk': nrm(17, (DEPTH, MLA_KV_RANK, MLA_HEADS, MLA_NOPE), MLA_KV_RANK ** -0.5),
        'mla_w_uv': nrm(18, (DEPTH, MLA_KV_RANK, MLA_HEADS, MLA_V), MLA_KV_RANK ** -0.5),
        'diff_lambda': nrm(19, (DEPTH, 4, DIFF_HALF), 0.1),
        'diff_subln': 1.0 + nrm(20, (DEPTH, HEAD_DIM), 0.02),
        'w_out': nrm(21, (DEPTH, MIX_WIDTH, D_MODEL), MIX_WIDTH ** -0.5 * DEEPNORM_BETA),
        'w_ada': nrm(22, (DEPTH, D_MODEL, 6 * D_MODEL), D_MODEL ** -0.5),
        'b_ada': nrm(23, (DEPTH, 6 * D_MODEL), 0.02),
        'ln_g': 1.0 + nrm(24, (DEPTH, 2, D_MODEL), 0.02),
        'ln_b': nrm(25, (DEPTH, 2, D_MODEL), 0.02),
        'router_w': nrm(26, (DEPTH, D_MODEL, N_EXPERTS), D_MODEL ** -0.5),
        'router_b': nrm(27, (DEPTH, N_EXPERTS), 0.01),
        'exp_w_gu': nrm(28, (DEPTH, N_EXPERTS, D_MODEL, 2 * D_FF), D_MODEL ** -0.5),
        'exp_b_gu': nrm(29, (DEPTH, N_EXPERTS, 2 * D_FF), 0.02),
        'exp_w_dn': nrm(30, (DEPTH, N_EXPERTS, D_FF, D_MODEL), D_FF ** -0.5 * DEEPNORM_BETA),
        'exp_b_dn': nrm(31, (DEPTH, N_EXPERTS, D_MODEL), 0.02),
        't5_table': nrm(32, (N_BUCKETS, DIFF_HEADS + DSA_HEADS), 0.5),
    }


def reference(x_prompt, x_sample, c_prompt, c_sample,
              cache_fox_kv, cache_fox_logf, cache_mla_ckv, cache_mla_krope,
              cache_diff_kv, cache_dsa_kv, cache_dsa_kidx, page_table,
              w_in, fox_f_bias, mla_q_norm, mla_w_uq, mla_kv_norm, mla_w_uk, mla_w_uv,
              diff_lambda, diff_subln, w_out, w_ada, b_ada, ln_g, ln_b,
              router_w, router_b, exp_w_gu, exp_b_gu, exp_w_dn, exp_b_dn, t5_table):
    past_len = page_table.shape[1] * cache_fox_kv.shape[2]
    t_prompt, t_sample = x_prompt.shape[1], x_sample.shape[1]
    pos_p = jnp.arange(t_prompt, dtype=jnp.int32)
    pos_s = past_len + jnp.arange(t_sample, dtype=jnp.int32)
    kpos_s = jnp.arange(past_len + t_sample, dtype=jnp.int32)
    caches = (cache_fox_kv, cache_fox_logf, cache_mla_ckv, cache_mla_krope,
              cache_diff_kv, cache_dsa_kv, cache_dsa_kidx)
    xp, xs = x_prompt, x_sample
    rows_p, rows_s = [], []
    for l in range(DEPTH):
        prm = {
            'w_in': w_in[l], 'fox_f_bias': fox_f_bias[l],
            'mla_q_norm': mla_q_norm[l], 'mla_w_uq': mla_w_uq[l], 'mla_kv_norm': mla_kv_norm[l],
            'mla_w_uk': mla_w_uk[l], 'mla_w_uv': mla_w_uv[l],
            'diff_lambda': diff_lambda[l], 'diff_subln': diff_subln[l],
            'w_out': w_out[l], 'w_ada': w_ada[l], 'b_ada': b_ada[l], 'ln_g': ln_g[l], 'ln_b': ln_b[l],
            'router_w': router_w[l], 'router_b': router_b[l],
            'exp_w_gu': exp_w_gu[l], 'exp_b_gu': exp_b_gu[l], 'exp_w_dn': exp_w_dn[l], 'exp_b_dn': exp_b_dn[l],
        }
        past = {n: gather_pages(cache, l, page_table) for n, cache in zip(CACHE_NAMES, caches)}
        xp, rp = trunk_layer(l, xp, c_prompt, pos_p, pos_p, None, prm, t5_table)
        xs, rs = trunk_layer(l, xs, c_sample, pos_s, kpos_s, past, prm, t5_table)
        rows_p.append(rp)
        rows_s.append(rs)

    def stack(rows, name):
        return jnp.stack([r[name] for r in rows], axis=0)

    return (xp, xs,
            stack(rows_p, 'fox_kv'), stack(rows_s, 'fox_kv'),
            stack(rows_p, 'fox_logf'), stack(rows_s, 'fox_logf'),
            stack(rows_p, 'mla_ckv'), stack(rows_s, 'mla_ckv'),
            stack(rows_p, 'mla_krope'), stack(rows_s, 'mla_krope'),
            stack(rows_p, 'diff_kv'), stack(rows_s, 'diff_kv'),
            stack(rows_p, 'dsa_kv'), stack(rows_s, 'dsa_kv'),
            stack(rows_p, 'dsa_kidx'), stack(rows_s, 'dsa_kidx'))
```

```python
import functools
import math

import numpy as np
import jax
import jax.numpy as jnp
from jax import lax
from jax.experimental import pallas as pl
from jax.experimental.pallas import tpu as pltpu

F32 = jnp.float32
BF16 = jnp.bfloat16
I32 = jnp.int32

LANES = 128
HEAD_DIM = 64
N_HEADS = 4
MLA_NOPE, MLA_ROPE, MLA_V, MLA_KV_RANK, MLA_Q_RANK = 64, 32, 64, 128, 192
DIFF_HALF = HEAD_DIM // 2
IDX_HEADS, IDX_DIM, IDX_TOPK = 8, 32, 256
ROPE_BASE = 10000.0
N_BUCKETS, MAX_DISTANCE = 32, 128
N_EXPERTS, TOP_K, EXPERT_BLOCK = 32, 4, 128
SWIGLU_LIMIT, SWIGLU_ALPHA = 7.0, 1.702
NEG_INF = -1e30
EPS = 1e-5
INT_MIN = -(2 ** 31)

IN_SIZES = (256, 64, 64, 4, 192, 128, 32, 256, 64, 64, 256, 64, 64, 256, 32, 8)
IN_OFFS = tuple(int(v) for v in np.cumsum((0,) + IN_SIZES))

S_FQ, S_FKV, S_FF, S_CQ, S_CKV, S_KR, S_DQ, S_DKV, S_SQ, S_SKV, S_IQ, S_IK, S_IW = (
    0, 4, 5, 6, 8, 9, 10, 18, 19, 23, 24, 32, 33)
N_SLABS = 34

ROW_TILE = 256
Q_TILE = 256
FINAL_TILE = 128
PAGES_PER_STEP = 8
VMEM_LIMIT = 56 * 1024 * 1024


def _cparams(sem, vmem=VMEM_LIMIT):
    return pltpu.CompilerParams(dimension_semantics=sem, vmem_limit_bytes=vmem)


def _bdot(a, b):
    return jnp.dot(a.astype(BF16), b.astype(BF16), preferred_element_type=F32)


def _bdot_nt(a, b):
    return lax.dot_general(a.astype(BF16), b.astype(BF16), (((1,), (1,)), ((), ())),
                           preferred_element_type=F32)


def _layer_norm(z, g, b):
    mu = jnp.mean(z, axis=-1, keepdims=True)
    var = jnp.mean(jnp.square(z - mu), axis=-1, keepdims=True)
    return (z - mu) * lax.rsqrt(var + EPS) * g + b


def _in_proj_columns():
    idx = np.full((N_SLABS * LANES,), -1, np.int64)

    def put(slab, lane0, src0, n):
        idx[slab * LANES + lane0: slab * LANES + lane0 + n] = np.arange(src0, src0 + n)

    o = IN_OFFS
    for h in range(N_HEADS):
        put(S_FQ + h, 0, o[0] + h * HEAD_DIM, HEAD_DIM)
        put(S_DQ + 2 * h, 0, o[7] + h * HEAD_DIM, DIFF_HALF)
        put(S_DQ + 2 * h + 1, DIFF_HALF, o[7] + h * HEAD_DIM + DIFF_HALF, DIFF_HALF)
        put(S_SQ + h, 0, o[10] + h * HEAD_DIM, HEAD_DIM)
    put(S_FKV, 0, o[1], 2 * HEAD_DIM)
    put(S_FF, 0, o[3], N_HEADS)
    put(S_CQ, 0, o[4], MLA_Q_RANK)
    put(S_CKV, 0, o[5], MLA_KV_RANK)
    put(S_KR, 0, o[6], MLA_ROPE)
    put(S_DKV, 0, o[8], 2 * HEAD_DIM)
    put(S_SKV, 0, o[11], 2 * HEAD_DIM)
    for h in range(IDX_HEADS):
        put(S_IQ + h, 0, o[13] + h * IDX_DIM, IDX_DIM)
    put(S_IK, 0, o[14], IDX_DIM)
    put(S_IW, 0, o[15], IDX_HEADS)
    return idx


_IN_COLS = _in_proj_columns()


def _pad_last(a, n):
    return jnp.pad(a, [(0, 0)] * (a.ndim - 1) + [(0, n - a.shape[-1])])


def _prep_layer_weights(l, w_in, fox_f_bias, mla_q_norm, mla_w_uq, mla_kv_norm, mla_w_uk, mla_w_uv,
                        diff_subln, w_out, router_w, router_b):
    cols = jnp.asarray(np.maximum(_IN_COLS, 0), I32)
    valid = jnp.asarray(_IN_COLS >= 0)
    w_in_p = jnp.where(valid[None, :], jnp.take(w_in[l], cols, axis=1), 0.0).astype(BF16)
    fb = _pad_last(fox_f_bias[l][None, :], LANES)
    gq = _pad_last(mla_q_norm[l][None, :], 2 * LANES)
    gkv = mla_kv_norm[l][None, :]
    wuq = mla_w_uq[l].reshape(MLA_Q_RANK, N_HEADS, MLA_NOPE + MLA_ROPE)
    nope = _pad_last(wuq[:, :, :MLA_NOPE], LANES).reshape(MLA_Q_RANK, N_HEADS * LANES)
    ropew = _pad_last(wuq[:, :, MLA_NOPE:], LANES).reshape(MLA_Q_RANK, N_HEADS * LANES)
    wuq_p = jnp.pad(jnp.concatenate([nope, ropew], axis=1), ((0, 2 * LANES - MLA_Q_RANK), (0, 0))).astype(BF16)
    wk = jnp.transpose(mla_w_uk[l], (1, 2, 0))
    wk = jnp.pad(wk, ((0, 0), (0, LANES - MLA_NOPE), (0, 0))).astype(BF16)
    wv = jnp.transpose(mla_w_uv[l], (1, 0, 2))
    wv = _pad_last(wv, LANES).astype(BF16)
    subln = jnp.pad(diff_subln[l][None, :], ((0, 0), (HEAD_DIM, 0)))
    rw = _pad_last(router_w[l], LANES)
    rb = jnp.pad(router_b[l][None, :], ((0, 0), (0, LANES - N_EXPERTS)), constant_values=-3e38)
    return dict(w_in_p=w_in_p, fb=fb, gq=gq, gkv=gkv, wuq_p=wuq_p, wk=wk, wv=wv, subln=subln, rw=rw, rb=rb)


def _prep_w_out(w_out_l, value_in_upper_half):
    w = w_out_l.reshape(4, N_HEADS, HEAD_DIM, -1)
    lo = jnp.pad(w, ((0, 0), (0, 0), (0, HEAD_DIM), (0, 0)))
    hi = jnp.pad(w, ((0, 0), (0, 0), (HEAD_DIM, 0), (0, 0)))
    sel = hi if value_in_upper_half else lo
    out = jnp.stack([sel[0], lo[1], sel[2], sel[3]], axis=0)
    return out.reshape(16, LANES, -1).astype(BF16)


def _t5_bucket(rel):
    n = jnp.maximum(rel, 0)
    max_exact = N_BUCKETS // 2
    nf = jnp.maximum(n, 1).astype(F32)
    large = max_exact + (jnp.log(nf / max_exact) / math.log(MAX_DISTANCE / max_exact)
                         * (N_BUCKETS - max_exact)).astype(I32)
    return jnp.where(n < max_exact, n, jnp.minimum(large, N_BUCKETS - 1))


def _t5_rows(table_cols, rel, reps):
    b = table_cols[_t5_bucket(rel)]
    b = jnp.moveaxis(b, -1, 0)
    b = b.reshape(N_HEADS * rel.shape[0], rel.shape[1])
    return jnp.concatenate([b] * reps, axis=0)


def _rope_tables(pos):
    half = MLA_ROPE // 2
    inv_freq = ROPE_BASE ** (-jnp.arange(half, dtype=F32) / half)
    ang = pos.astype(F32)[:, None] * inv_freq[None, :]
    cos, sin = jnp.cos(ang), jnp.sin(ang)
    cos_t = _pad_last(jnp.concatenate([cos, cos], axis=1), LANES)
    sin_t = _pad_last(jnp.concatenate([-sin, sin], axis=1), LANES)
    return cos_t, sin_t


def _ada_kernel(c_ref, w_ref, b_ref, o_ref):
    c = c_ref[...]
    o_ref[0] = _bdot(c * jax.nn.sigmoid(c), w_ref[0]) + b_ref[0]


def _ada(c_all, w_ada, b_ada):
    depth, d, n = w_ada.shape
    rows = c_all.shape[0]
    tn = n // 4 if n % (4 * LANES) == 0 else n
    return pl.pallas_call(
        _ada_kernel,
        grid=(depth, n // tn),
        in_specs=[pl.BlockSpec((rows, d), lambda l, j: (0, 0)),
                  pl.BlockSpec((1, d, tn), lambda l, j: (l, 0, j)),
                  pl.BlockSpec((1, 1, tn), lambda l, j: (l, 0, j))],
        out_specs=pl.BlockSpec((1, rows, tn), lambda l, j: (l, 0, j)),
        out_shape=jax.ShapeDtypeStruct((depth, rows, n), F32),
        compiler_params=_cparams(("arbitrary", "arbitrary")),
        name="ada_mod",
    )(c_all, w_ada, b_ada.reshape(depth, 1, n))


def _rope_apply(x, cos_t, sin_t):
    lane = lax.broadcasted_iota(I32, x.shape, 1)
    first_half = (lane % MLA_ROPE) < (MLA_ROPE // 2)
    swapped = jnp.where(first_half, pltpu.roll(x, LANES - MLA_ROPE // 2, 1), pltpu.roll(x, MLA_ROPE // 2, 1))
    return x * cos_t + swapped * sin_t


def _proj_kernel(x_ref, sc_ref, sh_ref, w_ref, fb_ref, gq_ref, wuq_ref, gkv_ref, wk_ref, cos_ref, sin_ref,
                 rows_ref, kvb_ref, qf_ref, qm_ref, qd_ref, qs_ref, qi_ref, iw_ref, cum_ref,
                 carry_ref, *, seq, tr):
    r = pl.program_id(0)
    h = x_ref[...] * (1.0 + sc_ref[0]) + sh_ref[0]
    p = _bdot(h, w_ref[...])

    def slab(i, n=1):
        return p[:, i * LANES:(i + n) * LANES]

    lane = lax.broadcasted_iota(I32, (tr, LANES), 1)
    ff = slab(S_FF) + fb_ref[...]
    logf = jnp.minimum(ff, 0.0) - jnp.log(1.0 + jnp.exp(-jnp.abs(ff)))
    logf = jnp.where(lane < N_HEADS, logf, 0.0)

    cq = slab(S_CQ, 2)
    cqn = cq * lax.rsqrt(jnp.sum(cq * cq, axis=-1, keepdims=True) * (1.0 / MLA_Q_RANK) + EPS) * gq_ref[...]
    q_mla = _bdot(cqn, wuq_ref[...])
    cos_t, sin_t = cos_ref[...], sin_ref[...]
    for hd in range(N_HEADS):
        q_nope = q_mla[:, hd * LANES:(hd + 1) * LANES]
        qm_ref[:, (2 * hd) * LANES:(2 * hd + 1) * LANES] = _bdot(q_nope, wk_ref[hd]).astype(BF16)
        q_rope = q_mla[:, (N_HEADS + hd) * LANES:(N_HEADS + hd + 1) * LANES]
        qm_ref[:, (2 * hd + 1) * LANES:(2 * hd + 2) * LANES] = _rope_apply(q_rope, cos_t, sin_t).astype(BF16)

    ckv = slab(S_CKV)
    ckvn = ckv * lax.rsqrt(jnp.mean(ckv * ckv, axis=-1, keepdims=True) + EPS) * gkv_ref[...]
    krope = _rope_apply(slab(S_KR), cos_t, sin_t)

    for i, v in enumerate((slab(S_FKV), logf, ckvn, krope, slab(S_DKV), slab(S_SKV), slab(S_IK))):
        rows_ref[:, i * LANES:(i + 1) * LANES] = v
    for i, v in enumerate((ckvn, krope, slab(S_FKV), slab(S_DKV), slab(S_SKV), slab(S_IK))):
        kvb_ref[:, i * LANES:(i + 1) * LANES] = v.astype(BF16)
    qf_ref[...] = slab(S_FQ, N_HEADS).astype(BF16)
    qd_ref[...] = slab(S_DQ, 2 * N_HEADS).astype(BF16)
    qs_ref[...] = slab(S_SQ, N_HEADS).astype(BF16)
    qi_ref[...] = slab(S_IQ, IDX_HEADS).astype(BF16)
    iw_ref[...] = slab(S_IW)

    ti = lax.broadcasted_iota(I32, (tr, tr), 0)
    tj = lax.broadcasted_iota(I32, (tr, tr), 1)
    if tr <= seq:
        tri = tj <= ti
    else:
        tri = (tj <= ti) & ((tj // seq) == (ti // seq))
    cum = jnp.dot(tri.astype(F32), logf, precision=lax.Precision.HIGHEST, preferred_element_type=F32)
    if tr <= seq:
        @pl.when(r % (seq // tr) == 0)
        def _():
            carry_ref[...] = jnp.zeros_like(carry_ref)
        cum = cum + carry_ref[0:1, :]
        carry_ref[...] = jnp.broadcast_to(cum[tr - 1:tr, :], carry_ref.shape)
    cum_ref[...] = cum


def _tile_mod(a, tr):
    return a if a.ndim == 3 else a.reshape(a.shape[0] // tr, tr, a.shape[1])


def _mod_map(mr, tr, rows_per_mod):
    if mr == 1:
        return lambda r, *_: (r * tr // rows_per_mod, 0, 0)
    return lambda r, *_: (r, 0, 0)


def _project(x2d, scale_b, shift_b, wts, cos_t, sin_t, seq, rows_per_mod):
    rtot, d = x2d.shape
    tr = min(ROW_TILE, rtot)
    assert rtot % tr == 0 and (seq % tr == 0 or tr % seq == 0)
    scale_b, shift_b = _tile_mod(scale_b, tr), _tile_mod(shift_b, tr)
    mr = scale_b.shape[1]
    mod_map = _mod_map(mr, tr, rows_per_mod)
    full = lambda a: pl.BlockSpec(a.shape, lambda r: (0,) * a.ndim)
    row = lambda n: pl.BlockSpec((tr, n), lambda r: (r, 0))
    outs = [(7 * LANES, F32), (6 * LANES, BF16), (4 * LANES, BF16), (8 * LANES, BF16), (8 * LANES, BF16),
            (4 * LANES, BF16), (8 * LANES, BF16), (LANES, F32), (LANES, F32)]
    return pl.pallas_call(
        functools.partial(_proj_kernel, seq=seq, tr=tr),
        grid=(rtot // tr,),
        in_specs=[row(d), pl.BlockSpec((1, mr, d), mod_map), pl.BlockSpec((1, mr, d), mod_map),
                  full(wts['w_in_p']), full(wts['fb']), full(wts['gq']), full(wts['wuq_p']), full(wts['gkv']),
                  full(wts['wk']), row(LANES), row(LANES)],
        out_specs=[row(n) for n, _ in outs],
        out_shape=[jax.ShapeDtypeStruct((rtot, n), dt) for n, dt in outs],
        scratch_shapes=[pltpu.VMEM((8, LANES), F32)],
        compiler_params=_cparams(("arbitrary",)),
        name="in_proj",
    )(x2d, scale_b, shift_b, wts['w_in_p'], wts['fb'], wts['gq'], wts['wuq_p'], wts['gkv'], wts['wk'],
      cos_t, sin_t)


def _online_update(s, v, m_ref, l_ref, acc_ref, keep=None):
    m_prev = m_ref[...]
    m_new = jnp.maximum(m_prev, jnp.max(s, axis=1, keepdims=True))
    alpha = jnp.exp(m_prev - m_new)
    p = jnp.exp(s - m_new)
    if keep is not None:
        p = jnp.where(keep, p, 0.0)
    l_ref[...] = alpha * l_ref[...] + jnp.sum(p, axis=1, keepdims=True)
    acc_ref[...] = alpha * acc_ref[...] + _bdot(p, v)
    m_ref[...] = m_new


def _causal(rows, tq, tk):
    ri = lax.broadcasted_iota(I32, (rows, tk), 0) % tq
    ci = lax.broadcasted_iota(I32, (rows, tk), 1)
    return ci <= ri


def _init_state(m_ref, l_ref, acc_ref):
    m_ref[...] = jnp.full(m_ref.shape, NEG_INF, F32)
    l_ref[...] = jnp.zeros(l_ref.shape, F32)
    acc_ref[...] = jnp.zeros(acc_ref.shape, F32)


def _fox_kernel(q_ref, kv_ref, cq_ref, ck_ref, o_ref, qs_ref, cc_ref, m_ref, l_ref, acc_ref, *, tq):
    i = pl.program_id(1)
    for h in range(N_HEADS):
        qs_ref[h * tq:(h + 1) * tq, :] = q_ref[:, h * LANES:(h + 1) * LANES]
        cc_ref[h * tq:(h + 1) * tq, :] = cq_ref[:, h:h + 1]
    _init_state(m_ref, l_ref, acc_ref)

    def step(j, diag):
        kv = kv_ref[pl.ds(pl.multiple_of(j * tq, tq), tq), :]
        s = _bdot_nt(qs_ref[...], kv) * HEAD_DIM ** -0.5
        ck = ck_ref[j]
        ckb = jnp.concatenate([jnp.broadcast_to(ck[h:h + 1, :], (tq, tq)) for h in range(N_HEADS)], axis=0)
        s = s + cc_ref[...] - ckb
        if diag:
            s = jnp.where(_causal(N_HEADS * tq, tq, tq), s, NEG_INF)
        _online_update(s, kv, m_ref, l_ref, acc_ref)

    def body(j, c):
        step(j, False)
        return c

    lax.fori_loop(0, i, body, 0)
    step(i, True)
    o = acc_ref[...] / l_ref[...]
    for h in range(N_HEADS):
        o_ref[h] = o[h * tq:(h + 1) * tq, :].astype(BF16)


def _mla_kernel(q_ref, k_ref, o_ref, qs_ref, m_ref, l_ref, acc_ref, *, tq):
    i = pl.program_id(1)
    for h in range(N_HEADS):
        qs_ref[h * tq:(h + 1) * tq, :] = q_ref[:, h * 2 * LANES:(h + 1) * 2 * LANES]
    _init_state(m_ref, l_ref, acc_ref)

    def step(j, diag):
        k = k_ref[pl.ds(pl.multiple_of(j * tq, tq), tq), :]
        s = _bdot_nt(qs_ref[...], k) * (MLA_NOPE + MLA_ROPE) ** -0.5
        if diag:
            s = jnp.where(_causal(N_HEADS * tq, tq, tq), s, NEG_INF)
        _online_update(s, k[:, :LANES], m_ref, l_ref, acc_ref)

    def body(j, c):
        step(j, False)
        return c

    lax.fori_loop(0, i, body, 0)
    step(i, True)
    o = acc_ref[...] / l_ref[...]
    for h in range(N_HEADS):
        o_ref[h] = o[h * tq:(h + 1) * tq, :].astype(BF16)


def _near_far_loop(i, step):
    def body(j, c):
        step(j, 2)
        return c

    lax.fori_loop(0, jnp.maximum(i - 1, 0), body, 0)

    @pl.when(i >= 1)
    def _():
        step(i - 1, 1)

    step(i, 0)


def _diff_kernel(lam_ref, q_ref, kv_ref, tb_ref, tfar_ref, g_ref, o_ref, qs_ref, m_ref, l_ref, acc_ref,
                 *, tq, lam_init):
    i = pl.program_id(1)
    nr = N_HEADS * tq
    for g in range(2):
        for h in range(N_HEADS):
            qs_ref[(g * N_HEADS + h) * tq:(g * N_HEADS + h + 1) * tq, :] = \
                q_ref[:, (2 * h + g) * LANES:(2 * h + g + 1) * LANES]
    _init_state(m_ref, l_ref, acc_ref)

    def step(j, dist):
        kv = kv_ref[pl.ds(pl.multiple_of(j * tq, tq), tq), :]
        s = _bdot_nt(qs_ref[...], kv) * DIFF_HALF ** -0.5
        if dist == 2:
            s = s + tfar_ref[...]
        else:
            s = s + tb_ref[dist]
        if dist == 0:
            s = jnp.where(_causal(2 * nr, tq, tq), s, NEG_INF)
        _online_update(s, kv, m_ref, l_ref, acc_ref)

    _near_far_loop(i, step)
    o = acc_ref[...] / l_ref[...]
    o = o[:nr] - lam_ref[0] * o[nr:]
    lane = lax.broadcasted_iota(I32, o.shape, 1)
    o = jnp.where(lane >= HEAD_DIM, o, 0.0)
    y = o * lax.rsqrt(jnp.sum(o * o, axis=-1, keepdims=True) * (1.0 / HEAD_DIM) + EPS) * g_ref[...]
    y = y * (1.0 - lam_init)
    for h in range(N_HEADS):
        o_ref[h] = y[h * tq:(h + 1) * tq, :].astype(BF16)


def _score_keys(sc):
    bits = lax.bitcast_convert_type(sc, I32)
    key = jnp.where(bits < 0, bits ^ jnp.int32(0x7FFFFFFF), bits)
    return jnp.where(sc == 0.0, 0, key)


def _dsa_kernel(qi_ref, iw_ref, ki_ref, q_ref, kv_ref, tb_ref, tfar_ref, o_ref,
                keys_ref, qis_ref, wc_ref, thr_ref, qs_ref, m_ref, l_ref, acc_ref, *, tq, nsel):
    i = pl.program_id(1)
    for h in range(IDX_HEADS):
        qis_ref[h * tq:(h + 1) * tq, :] = qi_ref[:, h * LANES:(h + 1) * LANES]
        wc_ref[h * tq:(h + 1) * tq, :] = iw_ref[:, h:h + 1] * IDX_HEADS ** -0.5
    for h in range(N_HEADS):
        qs_ref[h * tq:(h + 1) * tq, :] = q_ref[:, h * LANES:(h + 1) * LANES]
    _init_state(m_ref, l_ref, acc_ref)

    def score_tile(j, diag):
        kt = ki_ref[pl.ds(pl.multiple_of(j * tq, tq), tq), :]
        r = jnp.maximum(_bdot_nt(qis_ref[...], kt) * IDX_DIM ** -0.5, 0.0) * wc_ref[...]
        sc = r[0:tq]
        for h in range(1, IDX_HEADS):
            sc = sc + r[h * tq:(h + 1) * tq]
        key = _score_keys(sc)
        if diag:
            key = jnp.where(_causal(tq, tq, tq), key, INT_MIN)
        keys_ref[j] = key

    def sbody(j, c):
        score_tile(j, False)
        return c

    lax.fori_loop(0, i, sbody, 0)
    score_tile(i, True)

    def count_ge(cand):
        def cbody(j, c):
            ge = jnp.where(keys_ref[j] >= cand, 1.0, 0.0)
            part = ge[:, 0:LANES]
            for b in range(1, tq // LANES):
                part = part + ge[:, b * LANES:(b + 1) * LANES]
            return c + part

        c = lax.fori_loop(0, i + 1, cbody, jnp.zeros((tq, LANES), F32))
        return jnp.sum(c, axis=1, keepdims=True)

    zero = jnp.zeros((tq, 1), I32)
    t0 = jnp.where(count_ge(zero) >= nsel, zero, INT_MIN)

    def bit_body(b, t):
        cand = t + jnp.left_shift(jnp.int32(1), 30 - b)
        return jnp.where(count_ge(cand) >= nsel, cand, t)

    thr_ref[...] = lax.fori_loop(0, 31, bit_body, t0)

    def step(j, dist):
        kv = kv_ref[pl.ds(pl.multiple_of(j * tq, tq), tq), :]
        s = _bdot_nt(qs_ref[...], kv) * HEAD_DIM ** -0.5
        if dist == 2:
            s = s + tfar_ref[...]
        else:
            s = s + tb_ref[dist]
        sel = keys_ref[j] >= thr_ref[...]
        if dist == 0:
            sel = sel & _causal(tq, tq, tq)
        self32 = jnp.where(sel, 1.0, 0.0)
        keep = jnp.concatenate([self32] * N_HEADS, axis=0) > 0.5
        s = jnp.where(keep, s, NEG_INF)
        _online_update(s, kv, m_ref, l_ref, acc_ref, keep=keep)

    _near_far_loop(i, step)
    o = acc_ref[...] / l_ref[...]
    for h in range(N_HEADS):
        o_ref[h] = o[h * tq:(h + 1) * tq, :].astype(BF16)


def _prompt_attention(pr, cum, t5_table, lam, lam_init, subln, bsz, seq):
    rows, kvb, qf, qm, qd, qs, qi, iw, _ = pr
    tq = min(Q_TILE, seq)
    assert seq % tq == 0
    nt = seq // tq
    r3 = lambda a: a.reshape(bsz, seq, a.shape[-1])
    kvb3, qf3, qm3, qd3, qs3, qi3, iw3, cum3 = map(r3, (kvb, qf, qm, qd, qs, qi, iw, cum))
    ck = jnp.transpose(cum3[:, :, :8], (0, 2, 1)).reshape(bsz, 8, nt, tq).transpose(0, 2, 1, 3)

    qspec = lambda n: pl.BlockSpec((None, tq, n), lambda b, i: (b, i, 0))
    kspec = lambda n, blk: pl.BlockSpec((None, seq, n), lambda b, i: (b, 0, blk))
    ospec = pl.BlockSpec((None, N_HEADS, tq, LANES), lambda b, i: (b, 0, i, 0))
    oshape = jax.ShapeDtypeStruct((bsz, N_HEADS, seq, LANES), BF16)
    col = lambda n: pltpu.VMEM((n, 1), F32)
    grid = (bsz, nt)
    sem = ("arbitrary", "arbitrary")
    nr = N_HEADS * tq

    o_fox = pl.pallas_call(
        functools.partial(_fox_kernel, tq=tq), grid=grid,
        in_specs=[qspec(4 * LANES), kspec(LANES, 2), qspec(LANES),
                  pl.BlockSpec((None, nt, 8, tq), lambda b, i: (b, 0, 0, 0))],
        out_specs=ospec, out_shape=oshape,
        scratch_shapes=[pltpu.VMEM((nr, LANES), BF16), col(nr), col(nr), col(nr), pltpu.VMEM((nr, LANES), F32)],
        compiler_params=_cparams(sem), name="fox_attention",
    )(qf3, kvb3, cum3, ck)

    o_mla = pl.pallas_call(
        functools.partial(_mla_kernel, tq=tq), grid=grid,
        in_specs=[qspec(8 * LANES), kspec(2 * LANES, 0)],
        out_specs=ospec, out_shape=oshape,
        scratch_shapes=[pltpu.VMEM((nr, 2 * LANES), BF16), col(nr), col(nr), pltpu.VMEM((nr, LANES), F32)],
        compiler_params=_cparams(sem), name="mla_attention",
    )(qm3, kvb3)

    ri = jnp.arange(tq, dtype=I32)
    rel0 = ri[:, None] - ri[None, :]
    far_rel = jnp.full((1, 1), 2 * MAX_DISTANCE, I32)

    def t5_tiles(cols, reps):
        tb = jnp.stack([_t5_rows(cols, rel0, reps), _t5_rows(cols, rel0 + tq, reps)], axis=0)
        tfar = jnp.repeat(_t5_rows(cols, far_rel, reps), tq, axis=0)
        return tb.astype(F32), tfar.astype(F32)

    assert tq >= MAX_DISTANCE
    tb_d, tfar_d = t5_tiles(t5_table[:, :N_HEADS], 2)
    tb_s, tfar_s = t5_tiles(t5_table[:, N_HEADS:], 1)
    full = lambda a: pl.BlockSpec(a.shape, lambda b, i: (0,) * a.ndim)

    o_diff = pl.pallas_call(
        functools.partial(_diff_kernel, tq=tq, lam_init=lam_init), grid=grid,
        in_specs=[pl.BlockSpec(memory_space=pltpu.SMEM), qspec(8 * LANES), kspec(LANES, 3),
                  full(tb_d), full(tfar_d), full(subln)],
        out_specs=ospec, out_shape=oshape,
        scratch_shapes=[pltpu.VMEM((2 * nr, LANES), BF16), col(2 * nr), col(2 * nr),
                        pltpu.VMEM((2 * nr, LANES), F32)],
        compiler_params=_cparams(sem), name="diff_attention",
    )(lam.reshape(1), qd3, kvb3, tb_d, tfar_d, subln)

    nsel = min(IDX_TOPK, seq // 4)
    o_dsa = pl.pallas_call(
        functools.partial(_dsa_kernel, tq=tq, nsel=nsel), grid=grid,
        in_specs=[qspec(8 * LANES), qspec(LANES), kspec(LANES, 5), qspec(4 * LANES), kspec(LANES, 4),
                  full(tb_s), full(tfar_s)],
        out_specs=ospec, out_shape=oshape,
        scratch_shapes=[pltpu.VMEM((nt, tq, tq), I32), pltpu.VMEM((IDX_HEADS * tq, LANES), BF16),
                        col(IDX_HEADS * tq), pltpu.VMEM((tq, 1), I32),
                        pltpu.VMEM((nr, LANES), BF16), col(nr), col(nr), pltpu.VMEM((nr, LANES), F32)],
        compiler_params=_cparams(sem), name="dsa_attention",
    )(qi3, iw3, kvb3, qs3, kvb3, tb_s, tfar_s)
    return o_fox, o_mla, o_diff, o_dsa


def _out_kernel(x_ref, of_ref, om_ref, od_ref, os_ref, wo_ref, wv_ref, g1_ref, sc2_ref, sh2_ref,
                lng_ref, lnb_ref, rw_ref, rb_ref, x1_ref, h2_ref, ti_ref, tg_ref, *, tr, alpha):
    d = x_ref.shape[-1]
    acc = jnp.zeros((tr, d), F32)
    for h in range(N_HEADS):
        acc = acc + _bdot(of_ref[:, h].reshape(tr, LANES), wo_ref[h])
        o_mla = _bdot(om_ref[:, h].reshape(tr, LANES), wv_ref[h])
        acc = acc + _bdot(o_mla, wo_ref[N_HEADS + h])
        acc = acc + _bdot(od_ref[:, h].reshape(tr, LANES), wo_ref[2 * N_HEADS + h])
        acc = acc + _bdot(os_ref[:, h].reshape(tr, LANES), wo_ref[3 * N_HEADS + h])
    x1 = _layer_norm(alpha * x_ref[...] + g1_ref[0] * acc, lng_ref[...], lnb_ref[...])
    x1_ref[...] = x1
    h2 = x1 * (1.0 + sc2_ref[0]) + sh2_ref[0]
    h2_ref[...] = h2
    logits = jnp.dot(h2, rw_ref[...], precision=lax.Precision.HIGHEST, preferred_element_type=F32) + rb_ref[...]
    lane = lax.broadcasted_iota(I32, (tr, LANES), 1).astype(F32)
    idx_out = jnp.zeros((tr, LANES), F32)
    vals = []
    for k in range(TOP_K):
        mx = jnp.max(logits, axis=1, keepdims=True)
        ix = jnp.min(jnp.where(logits == mx, lane, float(LANES)), axis=1, keepdims=True)
        idx_out = jnp.where(lane == k, ix, idx_out)
        vals.append(mx)
        logits = jnp.where(lane == ix, -3.4e38, logits)
    es = [jnp.exp(v - vals[0]) for v in vals]
    den = es[0] + es[1] + es[2] + es[3]
    gates = jnp.zeros((tr, LANES), F32)
    for k in range(TOP_K):
        gates = jnp.where(lane == k, es[k] / den, gates)
    ti_ref[...] = idx_out.astype(I32)
    tg_ref[...] = gates


def _out_router(x2d, o_list, wo, wv, gate1, scale2, shift2, ln_g, ln_b, rw, rb, rows_per_mod, alpha):
    rtot, d = x2d.shape
    tr = min(ROW_TILE, rtot)
    nb_o, _, tt, _ = o_list[0].shape
    gate1, scale2, shift2 = _tile_mod(gate1, tr), _tile_mod(scale2, tr), _tile_mod(shift2, tr)
    mr = gate1.shape[1]
    mod_map = _mod_map(mr, tr, rows_per_mod)
    if tt >= tr:
        assert tt % tr == 0
        per = tt // tr
        ospec = pl.BlockSpec((1, N_HEADS, tr, LANES), lambda r: (r // per, 0, r % per, 0))
    else:
        assert tr % tt == 0
        ospec = pl.BlockSpec((tr // tt, N_HEADS, tt, LANES), lambda r: (r, 0, 0, 0))
    full = lambda a: pl.BlockSpec(a.shape, lambda r: (0,) * a.ndim)
    row = lambda n: pl.BlockSpec((tr, n), lambda r: (r, 0))
    mspec = pl.BlockSpec((1, mr, d), mod_map)
    return pl.pallas_call(
        functools.partial(_out_kernel, tr=tr, alpha=alpha),
        grid=(rtot // tr,),
        in_specs=[row(d), ospec, ospec, ospec, ospec, full(wo), full(wv), mspec, mspec, mspec,
                  full(ln_g), full(ln_b), full(rw), full(rb)],
        out_specs=[row(d), row(d), row(LANES), row(LANES)],
        out_shape=[jax.ShapeDtypeStruct((rtot, d), F32), jax.ShapeDtypeStruct((rtot, d), F32),
                   jax.ShapeDtypeStruct((rtot, LANES), I32), jax.ShapeDtypeStruct((rtot, LANES), F32)],
        compiler_params=_cparams(("arbitrary",)),
        name="out_proj_router",
    )(x2d, *o_list, wo, wv, gate1, scale2, shift2, ln_g, ln_b, rw, rb)


def _routing_tables(top_idx, n_tok):
    a = n_tok * TOP_K
    e_flat = top_idx.reshape(a)
    onehot = (e_flat[:, None] == jnp.arange(N_EXPERTS, dtype=I32)[None, :]).astype(I32)
    csum = jnp.cumsum(onehot, axis=0)
    rank = jnp.take_along_axis(csum, e_flat[:, None], axis=1)[:, 0] - 1
    counts = csum[-1]
    padded = (counts + EXPERT_BLOCK - 1) // EXPERT_BLOCK * EXPERT_BLOCK
    pstart = jnp.cumsum(padded) - padded
    dest = (pstart[e_flat] + rank).astype(I32)
    n_blocks = -(-a // EXPERT_BLOCK) + N_EXPERTS
    src = jnp.zeros((n_blocks * EXPERT_BLOCK,), I32).at[dest].set(jnp.arange(a, dtype=I32) // TOP_K)
    blk_expert = jnp.minimum(
        jnp.searchsorted(pstart + padded, jnp.arange(n_blocks, dtype=I32) * EXPERT_BLOCK, side='right'),
        N_EXPERTS - 1).astype(I32)
    n_used = (jnp.sum(padded) // EXPERT_BLOCK).astype(I32).reshape(1)
    return dest, src, blk_expert, n_used, n_blocks


def _moe_kernel(be_ref, src_ref, nu_ref, h_hbm, wgu_ref, bgu_ref, wdn_ref, bdn_ref, y_ref,
                xbuf, sem, wgu_b, wdn_b, *, nblk, dff):
    i = pl.program_id(0)
    slot = i % 2

    def row_copy(blk, sl, r):
        tok = src_ref[blk * EXPERT_BLOCK + r]
        return pltpu.make_async_copy(h_hbm.at[pl.ds(tok, 1)], xbuf.at[sl, pl.ds(r, 1)], sem.at[sl])

    def issue(blk, sl):
        def body(r, c):
            row_copy(blk, sl, r).start()
            return c
        lax.fori_loop(0, EXPERT_BLOCK, body, 0)

    @pl.when(i == 0)
    def _():
        issue(0, 0)

    @pl.when(i + 1 < nblk)
    def _():
        issue(i + 1, 1 - slot)

    def wbody(r, c):
        row_copy(i, slot, r).wait()
        return c
    lax.fori_loop(0, EXPERT_BLOCK, wbody, 0)

    prev = be_ref[jnp.maximum(i - 1, 0)]

    @pl.when((i == 0) | (be_ref[i] != prev))
    def _():
        wgu_b[...] = wgu_ref[0, 0].astype(BF16)
        wdn_b[...] = wdn_ref[0, 0].astype(BF16)

    @pl.when(i < nu_ref[0])
    def _():
        gu = jnp.dot(xbuf[slot].astype(BF16), wgu_b[...], preferred_element_type=F32) + bgu_ref[0, 0]
        g = jnp.minimum(gu[:, :dff], SWIGLU_LIMIT)
        u = jnp.clip(gu[:, dff:], -SWIGLU_LIMIT, SWIGLU_LIMIT)
        act = g * jax.nn.sigmoid(SWIGLU_ALPHA * g) * (u + 1.0)
        y_ref[...] = jnp.dot(act.astype(BF16), wdn_b[...], preferred_element_type=F32) + bdn_ref[0, 0]

    @pl.when(i >= nu_ref[0])
    def _():
        y_ref[...] = jnp.zeros_like(y_ref)


def _moe_blocks(l, h2, blk_expert, src, n_used, n_blocks, exp_w_gu, exp_b_gu, exp_w_dn, exp_b_dn):
    rtot, d = h2.shape
    dff = exp_w_dn.shape[2]
    bgu = exp_b_gu.reshape(exp_b_gu.shape[0], N_EXPERTS, 1, 2 * dff)
    bdn = exp_b_dn.reshape(exp_b_dn.shape[0], N_EXPERTS, 1, d)
    gs = pltpu.PrefetchScalarGridSpec(
        num_scalar_prefetch=3,
        grid=(n_blocks,),
        in_specs=[pl.BlockSpec(memory_space=pl.ANY),
                  pl.BlockSpec((1, 1, d, 2 * dff), lambda i, be, s, nu: (l, be[i], 0, 0)),
                  pl.BlockSpec((1, 1, 1, 2 * dff), lambda i, be, s, nu: (l, be[i], 0, 0)),
                  pl.BlockSpec((1, 1, dff, d), lambda i, be, s, nu: (l, be[i], 0, 0)),
                  pl.BlockSpec((1, 1, 1, d), lambda i, be, s, nu: (l, be[i], 0, 0))],
        out_specs=pl.BlockSpec((EXPERT_BLOCK, d), lambda i, be, s, nu: (i, 0)),
        scratch_shapes=[pltpu.VMEM((2, EXPERT_BLOCK, d), F32), pltpu.SemaphoreType.DMA((2,)),
                        pltpu.VMEM((d, 2 * dff), BF16), pltpu.VMEM((dff, d), BF16)])
    return pl.pallas_call(
        functools.partial(_moe_kernel, nblk=n_blocks, dff=dff),
        grid_spec=gs,
        out_shape=jax.ShapeDtypeStruct((n_blocks * EXPERT_BLOCK, d), F32),
        compiler_params=_cparams(("arbitrary",)),
        name="moe_experts",
    )(blk_expert, src, n_used, h2, exp_w_gu, bgu, exp_w_dn, bdn)


def _final_kernel(dest_ref, x1_ref, y_hbm, tg_ref, g2_ref, lng_ref, lnb_ref, o_ref, ybuf, sem,
                  *, nsteps, tr, alpha):
    i = pl.program_id(0)
    slot = i % 2

    def row_copy(step, sl, r, k):
        pos = dest_ref[(step * tr + r) * TOP_K + k]
        return pltpu.make_async_copy(y_hbm.at[pl.ds(pos, 1)], ybuf.at[sl, k, pl.ds(r, 1)], sem.at[sl])

    def issue(step, sl):
        def body(r, c):
            for k in range(TOP_K):
                row_copy(step, sl, r, k).start()
            return c
        lax.fori_loop(0, tr, body, 0)

    @pl.when(i == 0)
    def _():
        issue(0, 0)

    @pl.when(i + 1 < nsteps)
    def _():
        issue(i + 1, 1 - slot)

    def wbody(r, c):
        for k in range(TOP_K):
            row_copy(i, slot, r, k).wait()
        return c
    lax.fori_loop(0, tr, wbody, 0)

    tg = tg_ref[...]
    y = tg[:, 0:1] * ybuf[slot, 0]
    for k in range(1, TOP_K):
        y = y + tg[:, k:k + 1] * ybuf[slot, k]
    o_ref[...] = _layer_norm(alpha * x1_ref[...] + g2_ref[0] * y, lng_ref[...], lnb_ref[...])


def _combine_final(x1, y_pad, dest, gates, gate2, ln_g, ln_b, rows_per_mod, alpha):
    rtot, d = x1.shape
    tr = min(FINAL_TILE, rtot)
    nsteps = rtot // tr
    gate2 = _tile_mod(gate2, tr)
    mod_map = _mod_map(gate2.shape[1], tr, rows_per_mod)
    gs = pltpu.PrefetchScalarGridSpec(
        num_scalar_prefetch=1,
        grid=(nsteps,),
        in_specs=[pl.BlockSpec((tr, d), lambda r, dst: (r, 0)),
                  pl.BlockSpec(memory_space=pl.ANY),
                  pl.BlockSpec((tr, LANES), lambda r, dst: (r, 0)),
                  pl.BlockSpec((1, gate2.shape[1], d), mod_map),
                  pl.BlockSpec(ln_g.shape, lambda r, dst: (0, 0)),
                  pl.BlockSpec(ln_b.shape, lambda r, dst: (0, 0))],
        out_specs=pl.BlockSpec((tr, d), lambda r, dst: (r, 0)),
        scratch_shapes=[pltpu.VMEM((2, TOP_K, tr, d), F32), pltpu.SemaphoreType.DMA((2,))])
    return pl.pallas_call(
        functools.partial(_final_kernel, nsteps=nsteps, tr=tr, alpha=alpha),
        grid_spec=gs,
        out_shape=jax.ShapeDtypeStruct((rtot, d), F32),
        compiler_params=_cparams(("arbitrary",)),
        name="moe_combine_norm",
    )(dest, x1, y_pad, gates, gate2, ln_g, ln_b)


def _chunk_copies(specs, sem, layer, pt_ref, b, chunk, slot, pages):
    cps = []
    for ci, (cache, buf, key_major) in enumerate(specs):
        for p in range(pages):
            page = pt_ref[b, chunk * pages + p]
            src = cache.at[layer, page]
            if key_major:
                dst = buf.at[slot, pl.ds(p * LANES, LANES)]
            else:
                idx = (slot,) + (slice(None),) * (len(buf.shape) - 2) + (pl.ds(p * LANES, LANES),)
                dst = buf.at[idx]
            cps.append(pltpu.make_async_copy(src, dst, sem.at[slot, ci]))
    return cps


def _pipeline_fetch(specs, sem, layer, pt_ref, nch, pages, chunk_of):
    b, c = pl.program_id(0), pl.program_id(1)
    nb = pl.num_programs(0)
    g = b * nch + c
    slot = g % 2

    @pl.when(g == 0)
    def _():
        for cp in _chunk_copies(specs, sem, layer, pt_ref, b, chunk_of(c), slot, pages):
            cp.start()

    @pl.when(g + 1 < nb * nch)
    def _():
        g1 = g + 1
        b1, c1 = g1 // nch, g1 % nch
        for cp in _chunk_copies(specs, sem, layer, pt_ref, b1, chunk_of(c1), 1 - slot, pages):
            cp.start()

    for cp in _chunk_copies(specs, sem, layer, pt_ref, b, chunk_of(c), slot, pages):
        cp.wait()
    return slot


def _stack_slabs(x, n, width=1):
    return jnp.concatenate([x[:, h * width * LANES:(h + 1) * width * LANES] for h in range(n)], axis=0)


def _dec_idx_kernel(pt_ref, kidx_hbm, qi_ref, iw_ref, knew_ref, kp_ref, kn_ref, thr_ref,
                    kbuf, sem, allk, newk, *, layer, nch, pages, dec, nsel):
    c = pl.program_id(1)
    slot = _pipeline_fetch([(kidx_hbm, kbuf, False)], sem, layer, pt_ref, nch, pages, lambda cc: cc)
    q = _stack_slabs(qi_ref[...], IDX_HEADS)[:, :IDX_DIM]
    iw = iw_ref[...]
    wcol = jnp.concatenate([iw[:, h:h + 1] for h in range(IDX_HEADS)], axis=0) * IDX_HEADS ** -0.5

    def scores(kt):
        r = jnp.maximum(_bdot(q, kt) * IDX_DIM ** -0.5, 0.0) * wcol
        sc = r[0:dec]
        for h in range(1, IDX_HEADS):
            sc = sc + r[h * dec:(h + 1) * dec]
        return _score_keys(sc)

    key = scores(kbuf[slot])
    kp_ref[...] = key
    allk[c] = key

    @pl.when(c == nch - 1)
    def _():
        kn = scores(knew_ref[...])
        t = lax.broadcasted_iota(I32, kn.shape, 0)
        s = lax.broadcasted_iota(I32, kn.shape, 1)
        kn = jnp.where(s <= t, kn, INT_MIN)
        kn_ref[...] = kn
        newk[...] = kn

        def count_ge(cand):
            def cbody(j, acc):
                return acc + jnp.sum(jnp.where(allk[j] >= cand, 1.0, 0.0), axis=1, keepdims=True)
            acc = lax.fori_loop(0, nch, cbody, jnp.zeros((dec, 1), F32))
            return acc + jnp.sum(jnp.where(newk[...] >= cand, 1.0, 0.0), axis=1, keepdims=True)

        zero = jnp.zeros((dec, 1), I32)
        t0 = jnp.where(count_ge(zero) >= nsel, zero, INT_MIN)

        def bit_body(bi, tcur):
            cand = tcur + jnp.left_shift(jnp.int32(1), 30 - bi)
            return jnp.where(count_ge(cand) >= nsel, cand, tcur)

        thr = lax.fori_loop(0, 31, bit_body, t0)
        thr_ref[...] = jnp.broadcast_to(thr, thr_ref.shape)


def _suffix_sum(x):
    n = x.shape[1]
    lane = lax.broadcasted_iota(I32, x.shape, 1)
    d = 1
    while d < n:
        x = x + jnp.where(lane + d < n, pltpu.roll(x, n - d, 1), 0.0)
        d *= 2
    return x


def _online_update_fn(s, pv, m_ref, l_ref, acc_ref, keep=None):
    m_prev = m_ref[...]
    m_new = jnp.maximum(m_prev, jnp.max(s, axis=1, keepdims=True))
    alpha = jnp.exp(m_prev - m_new)
    p = jnp.exp(s - m_new)
    if keep is not None:
        p = jnp.where(keep, p, 0.0)
    l_ref[...] = alpha * l_ref[...] + jnp.sum(p, axis=1, keepdims=True)
    acc_ref[...] = alpha * acc_ref[...] + pv(p)
    m_ref[...] = m_new


def _dec_att_kernel(pt_ref, lam_ref,
                    fox_hbm, logf_hbm, ckv_hbm, kr_hbm, diff_hbm, dsa_hbm,
                    qf_ref, qm_ref, qd_ref, qs_ref, ecol_ref, kp_ref, kn_ref, thr_ref,
                    nf_ref, nl_ref, nc_ref, nk_ref, nd_ref, ns_ref,
                    tbd_ref, tbs_ref, tnd_ref, tns_ref, g_ref,
                    of_ref, om_ref, od_ref, os_ref,
                    fbuf, lbuf, cbuf, kbuf, dbuf, sbuf, sem,
                    mf, lf_, af, mm, lm, am, md, ld, ad, ms, ls, as_, rcar,
                    *, layer, nch, pages, dec, lam_init):
    c = pl.program_id(1)
    specs = [(fox_hbm, fbuf, False), (logf_hbm, lbuf, False), (ckv_hbm, cbuf, True), (kr_hbm, kbuf, False),
             (diff_hbm, dbuf, False), (dsa_hbm, sbuf, False)]
    slot = _pipeline_fetch(specs, sem, layer, pt_ref, nch, pages, lambda cc: nch - 1 - cc)
    nr = N_HEADS * dec

    qf = _stack_slabs(qf_ref[...], N_HEADS)[:, :HEAD_DIM]
    qm = _stack_slabs(qm_ref[...], N_HEADS, 2)
    qlat, qrope = qm[:, :LANES], qm[:, LANES:LANES + MLA_ROPE]
    qd_all = qd_ref[...]
    qd = jnp.concatenate([qd_all[:, (2 * h + g) * LANES:(2 * h + g + 1) * LANES]
                          for g in range(2) for h in range(N_HEADS)], axis=0)[:, :HEAD_DIM]
    qs = _stack_slabs(qs_ref[...], N_HEADS)[:, :HEAD_DIM]
    ecol = ecol_ref[...]
    thr = thr_ref[:, 0:1]

    def process(fkv, lgf, ckv, krt, dkv, skv, keys, t5d, t5s, is_new):
        nk = lgf.shape[1]
        if is_new:
            trow = lax.broadcasted_iota(I32, (dec, nk), 0)
            scol = lax.broadcasted_iota(I32, (dec, nk), 1)
            ok8 = jnp.where(scol <= trow, 1.0, 0.0)
            ok = jnp.concatenate([ok8] * N_HEADS, axis=0) > 0.5
            ok2 = jnp.concatenate([ok8] * (2 * N_HEADS), axis=0) > 0.5
        s = _bdot(qf, fkv[0]) * HEAD_DIM ** -0.5
        incl = _suffix_sum(lgf)
        excl = incl - lgf + rcar[...]
        bias = jnp.concatenate([jnp.broadcast_to(excl[h:h + 1, :], (dec, nk)) for h in range(N_HEADS)], axis=0)
        s = s + bias - ecol
        if is_new:
            s = jnp.where(ok, s, NEG_INF)
        _online_update_fn(s, lambda p: _bdot_nt(p, fkv[1]), mf, lf_, af)
        rcar[...] = rcar[...] + incl[:, 0:1]
        s = (_bdot_nt(qlat, ckv) + _bdot(qrope, krt)) * (MLA_NOPE + MLA_ROPE) ** -0.5
        if is_new:
            s = jnp.where(ok, s, NEG_INF)
        _online_update_fn(s, lambda p: _bdot(p, ckv), mm, lm, am)
        s = _bdot(qd, dkv[0]) * DIFF_HALF ** -0.5 + t5d
        if is_new:
            s = jnp.where(ok2, s, NEG_INF)
        _online_update_fn(s, lambda p: _bdot_nt(p, dkv[1]), md, ld, ad)
        s = _bdot(qs, skv[0]) * HEAD_DIM ** -0.5 + t5s
        sel8 = jnp.where(keys >= thr, 1.0, 0.0)
        if is_new:
            sel8 = sel8 * ok8
        keep = jnp.concatenate([sel8] * N_HEADS, axis=0) > 0.5
        s = jnp.where(keep, s, NEG_INF)
        _online_update_fn(s, lambda p: _bdot_nt(p, skv[1]), ms, ls, as_, keep=keep)

    @pl.when(c == 0)
    def _():
        for m_, l_, a_ in ((mf, lf_, af), (mm, lm, am), (md, ld, ad), (ms, ls, as_)):
            _init_state(m_, l_, a_)
        rcar[...] = jnp.zeros_like(rcar)
        process(nf_ref[...], nl_ref[...], nc_ref[...], nk_ref[...], nd_ref[...], ns_ref[...], kn_ref[...],
                tnd_ref[...], tns_ref[...], True)

    process(fbuf[slot], lbuf[slot], cbuf[slot], kbuf[slot], dbuf[slot], sbuf[slot], kp_ref[...],
            tbd_ref[...], tbs_ref[...], False)

    @pl.when(c == nch - 1)
    def _():
        def put(o_ref, o):
            w = o.shape[1]
            for h in range(N_HEADS):
                if w < LANES:
                    o_ref[h] = jnp.zeros((dec, LANES), BF16)
                    o_ref[h, :, 0:w] = o[h * dec:(h + 1) * dec].astype(BF16)
                else:
                    o_ref[h] = o[h * dec:(h + 1) * dec].astype(BF16)

        put(of_ref, af[...] / lf_[...])
        put(om_ref, am[...] / lm[...])
        od = ad[...] / ld[...]
        od = od[:nr] - lam_ref[0] * od[nr:]
        y = od * lax.rsqrt(jnp.mean(od * od, axis=-1, keepdims=True) + EPS) * g_ref[...] * (1.0 - lam_init)
        put(od_ref, y)
        put(os_ref, as_[...] / ls[...])


def _split_rows(rows, lead):
    s = lambda i, n: rows[:, i * LANES:i * LANES + n]
    return {
        'fox_kv': s(0, 2 * HEAD_DIM).reshape(lead + (2, HEAD_DIM)),
        'fox_logf': s(1, N_HEADS).reshape(lead + (N_HEADS,)),
        'mla_ckv': s(2, MLA_KV_RANK).reshape(lead + (MLA_KV_RANK,)),
        'mla_krope': s(3, MLA_ROPE).reshape(lead + (MLA_ROPE,)),
        'diff_kv': s(4, 2 * HEAD_DIM).reshape(lead + (2, HEAD_DIM)),
        'dsa_kv': s(5, 2 * HEAD_DIM).reshape(lead + (2, HEAD_DIM)),
        'dsa_kidx': s(6, IDX_DIM).reshape(lead + (IDX_DIM,)),
    }


def _ffn_and_norm(l, x1, h2, ti, tg, gate2, ln_g2, ln_b2, ew, rows_per_mod, alpha):
    n_tok = x1.shape[0]
    dest, src, blk_expert, n_used, n_blocks = _routing_tables(ti[:, :TOP_K], n_tok)
    y_pad = _moe_blocks(l, h2, blk_expert, src, n_used, n_blocks, *ew)
    return _combine_final(x1, y_pad, dest, tg, gate2, ln_g2, ln_b2, rows_per_mod, alpha)


def _lambda(l, diff_lambda):
    lam_init = 0.8 - 0.6 * math.exp(-0.3 * l)
    dl = diff_lambda[l]
    lam = (jnp.exp(jnp.sum(dl[0] * dl[1]).astype(F32)) - jnp.exp(jnp.sum(dl[2] * dl[3]).astype(F32)) + lam_init)
    return lam, lam_init


def _decode_attention(l, pr, caches_t, page_table, t5_table, lam, lam_init, subln_lo, bd, dec):
    rows, kvb, qf, qm, qd, qs, qi, iw, cum = pr
    fox_t, logf_t, ckv_c, kr_t, diff_t, dsa_t, kidx_t = caches_t
    n_pages = page_table.shape[1]
    pages = min(PAGES_PER_STEP, n_pages)
    assert n_pages % pages == 0
    nch = n_pages // pages
    gk = pages * LANES
    nk_past = n_pages * LANES
    nsel = min(IDX_TOPK, (nk_past + dec) // 4)

    rows3 = rows.reshape(bd, dec, rows.shape[-1])

    def key_minor(slab, n):
        x = jnp.transpose(rows3[:, :, slab * LANES:slab * LANES + n], (0, 2, 1))
        return _pad_last(x, LANES)

    nf = key_minor(0, 2 * HEAD_DIM).reshape(bd, 2, HEAD_DIM, LANES)
    nl = key_minor(1, N_HEADS)
    nc = jnp.pad(rows3[:, :, 2 * LANES:3 * LANES], ((0, 0), (0, LANES - dec), (0, 0)))
    nkr = key_minor(3, MLA_ROPE)
    nd = key_minor(4, 2 * HEAD_DIM).reshape(bd, 2, HEAD_DIM, LANES)
    ns = key_minor(5, 2 * HEAD_DIM).reshape(bd, 2, HEAD_DIM, LANES)
    nidx = key_minor(6, IDX_DIM)
    cum3 = cum.reshape(bd, dec, LANES)[:, :, :N_HEADS]
    ecol = jnp.transpose(cum3[:, dec - 1:dec, :] - cum3, (0, 2, 1)).reshape(bd, N_HEADS * dec, 1)

    sem = ("arbitrary", "arbitrary")
    row_blk = lambda n: pl.BlockSpec((dec, n), lambda b, c, *_: (b, 0))
    per_b = lambda a: pl.BlockSpec((None,) + a.shape[1:], lambda b, c, *_: (b,) + (0,) * (a.ndim - 1))
    anyspec = pl.BlockSpec(memory_space=pl.ANY)

    gs_idx = pltpu.PrefetchScalarGridSpec(
        num_scalar_prefetch=1, grid=(bd, nch),
        in_specs=[anyspec, row_blk(8 * LANES), row_blk(LANES), per_b(nidx)],
        out_specs=[pl.BlockSpec((None, dec, gk), lambda b, c, pt: (b, 0, c)),
                   pl.BlockSpec((None, dec, LANES), lambda b, c, pt: (b, 0, 0)),
                   pl.BlockSpec((None, dec, LANES), lambda b, c, pt: (b, 0, 0))],
        scratch_shapes=[pltpu.VMEM((2, IDX_DIM, gk), F32), pltpu.SemaphoreType.DMA((2, 1)),
                        pltpu.VMEM((nch, dec, gk), I32), pltpu.VMEM((dec, LANES), I32)])
    keys_past, keys_new, thr = pl.pallas_call(
        functools.partial(_dec_idx_kernel, layer=l, nch=nch, pages=pages, dec=dec, nsel=nsel),
        grid_spec=gs_idx,
        out_shape=[jax.ShapeDtypeStruct((bd, dec, nk_past), I32), jax.ShapeDtypeStruct((bd, dec, LANES), I32),
                   jax.ShapeDtypeStruct((bd, dec, LANES), I32)],
        compiler_params=_cparams(sem), name="decode_indexer",
    )(page_table, kidx_t, qi, iw, nidx)

    tpos = jnp.arange(dec, dtype=I32)[:, None]
    rel_last = gk + tpos - jnp.arange(gk, dtype=I32)[None, :]
    rel_far = jnp.full((dec, gk), 2 * MAX_DISTANCE, I32)
    rel_new = tpos - jnp.arange(LANES, dtype=I32)[None, :]

    def tiles(cols, reps):
        tb = jnp.stack([_t5_rows(cols, rel_far, reps), _t5_rows(cols, rel_last, reps)], axis=0).astype(F32)
        return tb, _t5_rows(cols, rel_new, reps).astype(F32)

    tbd, tnd = tiles(t5_table[:, :N_HEADS], 2)
    tbs, tns = tiles(t5_table[:, N_HEADS:], 1)
    recent = lambda a: pl.BlockSpec((None,) + a.shape[1:], lambda b, c, *_: (jnp.where(c == 0, 1, 0), 0, 0))
    full = lambda a: pl.BlockSpec(a.shape, lambda b, c, *_: (0,) * a.ndim)
    nr = N_HEADS * dec
    col = lambda n: pltpu.VMEM((n, 1), F32)
    ospec = pl.BlockSpec((None, N_HEADS, dec, LANES), lambda b, c, *_: (b, 0, 0, 0))
    oshape = jax.ShapeDtypeStruct((bd, N_HEADS, dec, LANES), BF16)
    gs_att = pltpu.PrefetchScalarGridSpec(
        num_scalar_prefetch=1, grid=(bd, nch),
        in_specs=[pl.BlockSpec(memory_space=pltpu.SMEM)] + [anyspec] * 6 + [
            row_blk(4 * LANES), row_blk(8 * LANES), row_blk(8 * LANES), row_blk(4 * LANES), per_b(ecol),
            pl.BlockSpec((None, dec, gk), lambda b, c, pt: (b, 0, nch - 1 - c)), per_b(keys_new), per_b(thr),
            per_b(nf), per_b(nl), per_b(nc), per_b(nkr), per_b(nd), per_b(ns),
            recent(tbd), recent(tbs), full(tnd), full(tns), full(subln_lo)],
        out_specs=[ospec] * 4,
        scratch_shapes=[pltpu.VMEM((2, 2, HEAD_DIM, gk), F32), pltpu.VMEM((2, N_HEADS, gk), F32),
                        pltpu.VMEM((2, gk, LANES), F32), pltpu.VMEM((2, MLA_ROPE, gk), F32),
                        pltpu.VMEM((2, 2, HEAD_DIM, gk), F32), pltpu.VMEM((2, 2, HEAD_DIM, gk), F32),
                        pltpu.SemaphoreType.DMA((2, 6)),
                        col(nr), col(nr), pltpu.VMEM((nr, HEAD_DIM), F32),
                        col(nr), col(nr), pltpu.VMEM((nr, LANES), F32),
                        col(2 * nr), col(2 * nr), pltpu.VMEM((2 * nr, HEAD_DIM), F32),
                        col(nr), col(nr), pltpu.VMEM((nr, HEAD_DIM), F32),
                        pltpu.VMEM((N_HEADS, 1), F32)])
    return pl.pallas_call(
        functools.partial(_dec_att_kernel, layer=l, nch=nch, pages=pages, dec=dec, lam_init=lam_init),
        grid_spec=gs_att,
        out_shape=[oshape] * 4,
        compiler_params=_cparams(sem), name="decode_attention",
    )(page_table, lam.reshape(1), fox_t, logf_t, ckv_c, kr_t, diff_t, dsa_t,
      qf, qm, qd, qs, ecol, keys_past, keys_new, thr, nf, nl, nc, nkr, nd, ns, tbd, tbs, tnd, tns, subln_lo)


def _sample_layer(l, x, mod, wts, w_out_l, ln_g, ln_b, ew, t5_table, diff_lambda, diff_subln, alpha,
                  caches_t, page_table):
    bd, dec, d = x.shape
    past_len = page_table.shape[1] * LANES
    mod_rows = jnp.repeat(mod, dec, axis=0)
    m = [mod_rows[:, k * d:(k + 1) * d] for k in range(6)]
    cos_t, sin_t = _rope_tables(past_len + jnp.arange(dec, dtype=I32))
    cos_t, sin_t = jnp.tile(cos_t, (bd, 1)), jnp.tile(sin_t, (bd, 1))
    x2d = x.reshape(bd * dec, d)
    pr = _project(x2d, m[1], m[0], wts, cos_t, sin_t, dec, 1)
    lam, lam_init = _lambda(l, diff_lambda)
    o_list = _decode_attention(l, pr, caches_t, page_table, t5_table, lam, lam_init, diff_subln[l][None, :], bd, dec)
    wo = _prep_w_out(w_out_l, False)
    x1, h2, ti, tg = _out_router(x2d, o_list, wo, wts['wv'], m[2], m[4], m[3], ln_g[l, 0:1], ln_b[l, 0:1],
                                 wts['rw'], wts['rb'], 1, alpha)
    x2 = _ffn_and_norm(l, x1, h2, ti, tg, m[5], ln_g[l, 1:2], ln_b[l, 1:2], ew, 1, alpha)
    return x2.reshape(bd, dec, d), _split_rows(pr[0], (bd, dec))


def _prompt_layer(l, x, mod, wts, w_out_l, ln_g, ln_b, ew, t5_table, diff_lambda, alpha):
    bsz, seq, d = x.shape
    m = [mod[:, None, k * d:(k + 1) * d] for k in range(6)]
    cos_t, sin_t = _rope_tables(jnp.arange(seq, dtype=I32))
    cos_t, sin_t = jnp.tile(cos_t, (bsz, 1)), jnp.tile(sin_t, (bsz, 1))
    x2d = x.reshape(bsz * seq, d)
    pr = _project(x2d, m[1], m[0], wts, cos_t, sin_t, seq, seq)
    lam, lam_init = _lambda(l, diff_lambda)
    o_list = _prompt_attention(pr, pr[8], t5_table, lam, lam_init, wts['subln'], bsz, seq)
    wo = _prep_w_out(w_out_l, True)
    x1, h2, ti, tg = _out_router(x2d, o_list, wo, wts['wv'], m[2], m[4], m[3], ln_g[l, 0:1], ln_b[l, 0:1],
                                 wts['rw'], wts['rb'], seq, alpha)
    x2 = _ffn_and_norm(l, x1, h2, ti, tg, m[5], ln_g[l, 1:2], ln_b[l, 1:2], ew, seq, alpha)
    return x2.reshape(bsz, seq, d), _split_rows(pr[0], (bsz, seq))


def kernel(x_prompt, x_sample, c_prompt, c_sample, cache_fox_kv, cache_fox_logf, cache_mla_ckv, cache_mla_krope,
           cache_diff_kv, cache_dsa_kv, cache_dsa_kidx, page_table, w_in, fox_f_bias, mla_q_norm, mla_w_uq,
           mla_kv_norm, mla_w_uk, mla_w_uv, diff_lambda, diff_subln, w_out, w_ada, b_ada, ln_g, ln_b,
           router_w, router_b, exp_w_gu, exp_b_gu, exp_w_dn, exp_b_dn, t5_table):
    depth = w_in.shape[0]
    alpha = (2 * depth) ** 0.25
    bsz, bd = x_prompt.shape[0], x_sample.shape[0]
    caches_t = (jnp.transpose(cache_fox_kv, (0, 1, 3, 4, 2)), jnp.transpose(cache_fox_logf, (0, 1, 3, 2)),
                cache_mla_ckv, jnp.transpose(cache_mla_krope, (0, 1, 3, 2)),
                jnp.transpose(cache_diff_kv, (0, 1, 3, 4, 2)), jnp.transpose(cache_dsa_kv, (0, 1, 3, 4, 2)),
                jnp.transpose(cache_dsa_kidx, (0, 1, 3, 2)))
    n_c = bsz + bd
    c_all = jnp.pad(jnp.concatenate([c_prompt, c_sample], axis=0), ((0, -n_c % 8), (0, 0)))
    mod_all = _ada(c_all, w_ada, b_ada)
    ew = (exp_w_gu, exp_b_gu, exp_w_dn, exp_b_dn)
    xp, xs = x_prompt, x_sample
    rows_p, rows_s = [], []
    for l in range(depth):
        wts = _prep_layer_weights(l, w_in, fox_f_bias, mla_q_norm, mla_w_uq, mla_kv_norm, mla_w_uk, mla_w_uv,
                                  diff_subln, w_out, router_w, router_b)
        xp, rp = _prompt_layer(l, xp, mod_all[l, :bsz], wts, w_out[l], ln_g, ln_b, ew, t5_table, diff_lambda, alpha)
        xs, rs = _sample_layer(l, xs, mod_all[l, bsz:n_c], wts, w_out[l], ln_g, ln_b, ew, t5_table, diff_lambda,
                               diff_subln, alpha, caches_t, page_table)
        rows_p.append(rp)
        rows_s.append(rs)

    def stack(rows, name):
        return jnp.stack([r[name] for r in rows], axis=0)

    out = [xp, xs]
    for name in ('fox_kv', 'fox_logf', 'mla_ckv', 'mla_krope', 'diff_kv', 'dsa_kv', 'dsa_kidx'):
        out += [stack(rows_p, name), stack(rows_s, name)]
    return tuple(out)
```

```python
import functools
import math

import numpy as np
import jax
import jax.numpy as jnp
from jax import lax
from jax.experimental import pallas as pl
from jax.experimental.pallas import tpu as pltpu

F32 = jnp.float32
BF16 = jnp.bfloat16
I32 = jnp.int32

LANES = 128
HEAD_DIM = 64
N_HEADS = 4
MLA_NOPE, MLA_ROPE, MLA_V, MLA_KV_RANK, MLA_Q_RANK = 64, 32, 64, 128, 192
DIFF_HALF = HEAD_DIM // 2
IDX_HEADS, IDX_DIM, IDX_TOPK = 8, 32, 256
ROPE_BASE = 10000.0
N_BUCKETS, MAX_DISTANCE = 32, 128
N_EXPERTS, TOP_K, EXPERT_BLOCK = 32, 4, 128
SWIGLU_LIMIT, SWIGLU_ALPHA = 7.0, 1.702
NEG_INF = -1e30
EPS = 1e-5
INT_MIN = -(2 ** 31)

IN_SIZES = (256, 64, 64, 4, 192, 128, 32, 256, 64, 64, 256, 64, 64, 256, 32, 8)
IN_OFFS = tuple(int(v) for v in np.cumsum((0,) + IN_SIZES))

S_FQ, S_FKV, S_FF, S_CQ, S_CKV, S_KR, S_DQ, S_DKV, S_SQ, S_SKV, S_IQ, S_IK, S_IW = (
    0, 4, 5, 6, 8, 9, 10, 18, 19, 23, 24, 32, 33)
N_SLABS = 34

ROW_TILE = 256
Q_TILE = 256
FINAL_TILE = 128
PAGES_PER_STEP = 32
VMEM_LIMIT = 56 * 1024 * 1024


def _cparams(sem, vmem=VMEM_LIMIT):
    return pltpu.CompilerParams(dimension_semantics=sem, vmem_limit_bytes=vmem)


def _bdot(a, b):
    return jnp.dot(a.astype(BF16), b.astype(BF16), preferred_element_type=F32)


def _bdot_nt(a, b):
    return lax.dot_general(a.astype(BF16), b.astype(BF16), (((1,), (1,)), ((), ())),
                           preferred_element_type=F32)


def _layer_norm(z, g, b):
    mu = jnp.mean(z, axis=-1, keepdims=True)
    var = jnp.mean(jnp.square(z - mu), axis=-1, keepdims=True)
    return (z - mu) * lax.rsqrt(var + EPS) * g + b


def _in_proj_columns():
    idx = np.full((N_SLABS * LANES,), -1, np.int64)

    def put(slab, lane0, src0, n):
        idx[slab * LANES + lane0: slab * LANES + lane0 + n] = np.arange(src0, src0 + n)

    o = IN_OFFS
    for h in range(N_HEADS):
        put(S_FQ + h, 0, o[0] + h * HEAD_DIM, HEAD_DIM)
        put(S_DQ + 2 * h, 0, o[7] + h * HEAD_DIM, DIFF_HALF)
        put(S_DQ + 2 * h + 1, DIFF_HALF, o[7] + h * HEAD_DIM + DIFF_HALF, DIFF_HALF)
        put(S_SQ + h, 0, o[10] + h * HEAD_DIM, HEAD_DIM)
    put(S_FKV, 0, o[1], 2 * HEAD_DIM)
    put(S_FF, 0, o[3], N_HEADS)
    put(S_CQ, 0, o[4], MLA_Q_RANK)
    put(S_CKV, 0, o[5], MLA_KV_RANK)
    put(S_KR, 0, o[6], MLA_ROPE)
    put(S_DKV, 0, o[8], 2 * HEAD_DIM)
    put(S_SKV, 0, o[11], 2 * HEAD_DIM)
    for h in range(IDX_HEADS):
        put(S_IQ + h, 0, o[13] + h * IDX_DIM, IDX_DIM)
    put(S_IK, 0, o[14], IDX_DIM)
    put(S_IW, 0, o[15], IDX_HEADS)
    return idx


def _column_runs(idx):
    runs, i = [], 0
    while i < len(idx):
        j = i + 1
        if idx[i] < 0:
            while j < len(idx) and idx[j] < 0:
                j += 1
            runs.append((0, j - i, True))
        else:
            while j < len(idx) and idx[j] == idx[j - 1] + 1:
                j += 1
            runs.append((int(idx[i]), j - i, False))
        i = j
    return tuple(runs)


_IN_RUNS = _column_runs(_in_proj_columns())


def _pad_last(a, n):
    return jnp.pad(a, [(0, 0)] * (a.ndim - 1) + [(0, n - a.shape[-1])])


def _prep_layer_weights(l, w_in, fox_f_bias, mla_q_norm, mla_w_uq, mla_kv_norm, mla_w_uk, mla_w_uv,
                        diff_subln, w_out, router_w, router_b):
    pieces = []
    for start, n, is_zero in _IN_RUNS:
        if is_zero:
            pieces.append(jnp.zeros((w_in.shape[1], n), w_in.dtype))
        else:
            pieces.append(w_in[l, :, start:start + n])
    w_in_p = jnp.concatenate(pieces, axis=1).astype(BF16)
    fb = _pad_last(fox_f_bias[l][None, :], LANES)
    gq = _pad_last(mla_q_norm[l][None, :], 2 * LANES)
    gkv = mla_kv_norm[l][None, :]
    wuq = mla_w_uq[l].reshape(MLA_Q_RANK, N_HEADS, MLA_NOPE + MLA_ROPE)
    nope = _pad_last(wuq[:, :, :MLA_NOPE], LANES).reshape(MLA_Q_RANK, N_HEADS * LANES)
    ropew = _pad_last(wuq[:, :, MLA_NOPE:], LANES).reshape(MLA_Q_RANK, N_HEADS * LANES)
    wuq_p = jnp.pad(jnp.concatenate([nope, ropew], axis=1), ((0, 2 * LANES - MLA_Q_RANK), (0, 0))).astype(BF16)
    wk = jnp.transpose(mla_w_uk[l], (1, 2, 0))
    wk = jnp.pad(wk, ((0, 0), (0, LANES - MLA_NOPE), (0, 0))).astype(BF16)
    wv = jnp.transpose(mla_w_uv[l], (1, 0, 2))
    wv = _pad_last(wv, LANES).astype(BF16)
    subln = jnp.pad(diff_subln[l][None, :], ((0, 0), (HEAD_DIM, 0)))
    rw = _pad_last(router_w[l], LANES)
    rb = jnp.pad(router_b[l][None, :], ((0, 0), (0, LANES - N_EXPERTS)), constant_values=-3e38)
    return dict(w_in_p=w_in_p, fb=fb, gq=gq, gkv=gkv, wuq_p=wuq_p, wk=wk, wv=wv, subln=subln, rw=rw, rb=rb)


def _prep_w_out(w_out_l, value_in_upper_half):
    w = w_out_l.reshape(4, N_HEADS, HEAD_DIM, -1)
    lo = jnp.pad(w, ((0, 0), (0, 0), (0, HEAD_DIM), (0, 0)))
    hi = jnp.pad(w, ((0, 0), (0, 0), (HEAD_DIM, 0), (0, 0)))
    sel = hi if value_in_upper_half else lo
    out = jnp.stack([sel[0], lo[1], sel[2], sel[3]], axis=0)
    return out.reshape(16, LANES, -1).astype(BF16)


def _t5_bucket(rel):
    n = jnp.maximum(rel, 0)
    max_exact = N_BUCKETS // 2
    nf = jnp.maximum(n, 1).astype(F32)
    large = max_exact + (jnp.log(nf / max_exact) / math.log(MAX_DISTANCE / max_exact)
                         * (N_BUCKETS - max_exact)).astype(I32)
    return jnp.where(n < max_exact, n, jnp.minimum(large, N_BUCKETS - 1))


def _t5_rows(table_cols, rel, reps):
    b = table_cols[_t5_bucket(rel)]
    b = jnp.moveaxis(b, -1, 0)
    b = b.reshape(N_HEADS * rel.shape[0], rel.shape[1])
    return jnp.concatenate([b] * reps, axis=0)


def _rope_tables(pos):
    half = MLA_ROPE // 2
    inv_freq = ROPE_BASE ** (-jnp.arange(half, dtype=F32) / half)
    ang = pos.astype(F32)[:, None] * inv_freq[None, :]
    cos, sin = jnp.cos(ang), jnp.sin(ang)
    cos_t = _pad_last(jnp.concatenate([cos, cos], axis=1), LANES)
    sin_t = _pad_last(jnp.concatenate([-sin, sin], axis=1), LANES)
    return cos_t, sin_t


def _ada_kernel(c_ref, w_ref, b_ref, o_ref):
    c = c_ref[...]
    o_ref[0] = _bdot(c * jax.nn.sigmoid(c), w_ref[0]) + b_ref[0]


def _ada(c_all, w_ada, b_ada):
    depth, d, n = w_ada.shape
    rows = c_all.shape[0]
    tn = n // 4 if n % (4 * LANES) == 0 else n
    return pl.pallas_call(
        _ada_kernel,
        grid=(depth, n // tn),
        in_specs=[pl.BlockSpec((rows, d), lambda l, j: (0, 0)),
                  pl.BlockSpec((1, d, tn), lambda l, j: (l, 0, j)),
                  pl.BlockSpec((1, 1, tn), lambda l, j: (l, 0, j))],
        out_specs=pl.BlockSpec((1, rows, tn), lambda l, j: (l, 0, j)),
        out_shape=jax.ShapeDtypeStruct((depth, rows, n), F32),
        compiler_params=_cparams(("arbitrary", "arbitrary")),
        name="ada_mod",
    )(c_all, w_ada, b_ada.reshape(depth, 1, n))


def _rope_apply(x, cos_t, sin_t):
    lane = lax.broadcasted_iota(I32, x.shape, 1)
    first_half = (lane % MLA_ROPE) < (MLA_ROPE // 2)
    swapped = jnp.where(first_half, pltpu.roll(x, LANES - MLA_ROPE // 2, 1), pltpu.roll(x, MLA_ROPE // 2, 1))
    return x * cos_t + swapped * sin_t


def _proj_kernel(*refs, seq, tr, transposed):
    (x_ref, sc_ref, sh_ref, w_ref, fb_ref, gq_ref, wuq_ref, gkv_ref, wk_ref, cos_ref, sin_ref) = refs[:11]
    out_refs, carry_ref = refs[11:-1], refs[-1]
    r = pl.program_id(0)
    h = x_ref[...] * (1.0 + sc_ref[0]) + sh_ref[0]
    p = _bdot(h, w_ref[...])

    def slab(i, n=1):
        return p[:, i * LANES:(i + n) * LANES]

    lane = lax.broadcasted_iota(I32, (tr, LANES), 1)
    ff = slab(S_FF) + fb_ref[...]
    logf = jnp.minimum(ff, 0.0) - jnp.log(1.0 + jnp.exp(-jnp.abs(ff)))
    logf = jnp.where(lane < N_HEADS, logf, 0.0)

    cq = slab(S_CQ, 2)
    cqn = cq * lax.rsqrt(jnp.sum(cq * cq, axis=-1, keepdims=True) * (1.0 / MLA_Q_RANK) + EPS) * gq_ref[...]
    q_mla = _bdot(cqn, wuq_ref[...])
    cos_t, sin_t = cos_ref[...], sin_ref[...]
    qm_slabs = []
    for hd in range(N_HEADS):
        q_nope = q_mla[:, hd * LANES:(hd + 1) * LANES]
        qm_slabs.append(_bdot(q_nope, wk_ref[hd]))
        q_rope = q_mla[:, (N_HEADS + hd) * LANES:(N_HEADS + hd + 1) * LANES]
        qm_slabs.append(_rope_apply(q_rope, cos_t, sin_t))

    ckv = slab(S_CKV)
    ckvn = ckv * lax.rsqrt(jnp.mean(ckv * ckv, axis=-1, keepdims=True) + EPS) * gkv_ref[...]
    krope = _rope_apply(slab(S_KR), cos_t, sin_t)

    rows_ref = out_refs[0]
    for i, v in enumerate((slab(S_FKV), logf, ckvn, krope, slab(S_DKV), slab(S_SKV), slab(S_IK))):
        rows_ref[:, i * LANES:(i + 1) * LANES] = v

    ti = lax.broadcasted_iota(I32, (tr, tr), 0)
    tj = lax.broadcasted_iota(I32, (tr, tr), 1)
    if tr <= seq:
        tri = tj <= ti
    else:
        tri = (tj <= ti) & ((tj // seq) == (ti // seq))
    cum = jnp.dot(tri.astype(F32), logf, precision=lax.Precision.HIGHEST, preferred_element_type=F32)
    if tr <= seq:
        @pl.when(r % (seq // tr) == 0)
        def _():
            carry_ref[...] = jnp.zeros_like(carry_ref)
        cum = cum + carry_ref[0:1, :]
        carry_ref[...] = jnp.broadcast_to(cum[tr - 1:tr, :], carry_ref.shape)
    out_refs[1][...] = cum

    if not transposed:
        qf_ref, qm_ref, qd_ref, qs_ref, qi_ref, iw_ref = out_refs[2:]
        qf_ref[...] = slab(S_FQ, N_HEADS).astype(BF16)
        for i, v in enumerate(qm_slabs):
            qm_ref[:, i * LANES:(i + 1) * LANES] = v.astype(BF16)
        qd_ref[...] = slab(S_DQ, 2 * N_HEADS).astype(BF16)
        qs_ref[...] = slab(S_SQ, N_HEADS).astype(BF16)
        qi_ref[...] = slab(S_IQ, IDX_HEADS).astype(BF16)
        iw_ref[...] = slab(S_IW)
        return

    kvb_ref, vt_ref, qt_ref, aux_ref = out_refs[2:]
    hi = cum.astype(BF16).astype(F32)
    mid = (cum - hi).astype(BF16).astype(F32)
    lo = cum - hi - mid
    ones = jnp.where((lane >= 3 * N_HEADS) & (lane < 3 * N_HEADS + 3), 1.0, 0.0)
    aug = hi + pltpu.roll(mid, N_HEADS, 1) + pltpu.roll(lo, 2 * N_HEADS, 1) + ones
    for i, v in enumerate((ckvn, krope, slab(S_FKV), aug, slab(S_DKV), slab(S_SKV), slab(S_IK))):
        kvb_ref[:, i * LANES:(i + 1) * LANES] = v.astype(BF16)
    for i, v in enumerate((ckvn, slab(S_FKV), slab(S_DKV), slab(S_SKV))):
        vt_ref[0, i * LANES:(i + 1) * LANES, :] = v.T.astype(BF16)
    q_list = (qm_slabs + [slab(S_DQ + i) for i in range(2 * N_HEADS)] + [slab(S_IQ + i) for i in range(IDX_HEADS)]
              + [slab(S_FQ + i) * HEAD_DIM ** -0.5 for i in range(N_HEADS)]
              + [slab(S_SQ + i) * HEAD_DIM ** -0.5 for i in range(N_HEADS)])
    for i, v in enumerate(q_list):
        qt_ref[0, i * LANES:(i + 1) * LANES, :] = v.T.astype(BF16)
    aux_ref[0, 0:LANES, :] = cum.T
    aux_ref[0, LANES:2 * LANES, :] = slab(S_IW).T


def _tile_mod(a, tr):
    return a if a.ndim == 3 else a.reshape(a.shape[0] // tr, tr, a.shape[1])


def _mod_map(mr, tr, rows_per_mod):
    if mr == 1:
        return lambda r, *_: (r * tr // rows_per_mod, 0, 0)
    return lambda r, *_: (r, 0, 0)


def _project(x2d, scale_b, shift_b, wts, cos_t, sin_t, seq, rows_per_mod, transposed):
    rtot, d = x2d.shape
    tr = min(ROW_TILE, rtot)
    assert rtot % tr == 0 and (seq % tr == 0 or tr % seq == 0)
    scale_b, shift_b = _tile_mod(scale_b, tr), _tile_mod(shift_b, tr)
    mr = scale_b.shape[1]
    mod_map = _mod_map(mr, tr, rows_per_mod)
    full = lambda a: pl.BlockSpec(a.shape, lambda r: (0,) * a.ndim)
    row = lambda n: pl.BlockSpec((tr, n), lambda r: (r, 0))
    outs = [(7 * LANES, F32), (LANES, F32)]
    if transposed:
        outs += [(7 * LANES, BF16)]
        touts = [(4 * LANES, BF16), (32 * LANES, BF16), (2 * LANES, F32)]
    else:
        outs += [(4 * LANES, BF16), (8 * LANES, BF16), (8 * LANES, BF16), (4 * LANES, BF16), (8 * LANES, BF16),
                 (LANES, F32)]
        touts = []
    return pl.pallas_call(
        functools.partial(_proj_kernel, seq=seq, tr=tr, transposed=transposed),
        grid=(rtot // tr,),
        in_specs=[row(d), pl.BlockSpec((1, mr, d), mod_map), pl.BlockSpec((1, mr, d), mod_map),
                  full(wts['w_in_p']), full(wts['fb']), full(wts['gq']), full(wts['wuq_p']), full(wts['gkv']),
                  full(wts['wk']), row(LANES), row(LANES)],
        out_specs=[row(n) for n, _ in outs] + [pl.BlockSpec((1, n, tr), lambda r: (r, 0, 0)) for n, _ in touts],
        out_shape=[jax.ShapeDtypeStruct((rtot, n), dt) for n, dt in outs]
        + [jax.ShapeDtypeStruct((rtot // tr, n, tr), dt) for n, dt in touts],
        scratch_shapes=[pltpu.VMEM((8, LANES), F32)],
        compiler_params=_cparams(("arbitrary",)),
        name="in_proj",
    )(x2d, scale_b, shift_b, wts['w_in_p'], wts['fb'], wts['gq'], wts['wuq_p'], wts['gkv'], wts['wk'],
      cos_t, sin_t)


def _col_online(st, vt, m_ref, l_ref, acc_ref, keep=None):
    m_prev = m_ref[...]
    m_new = jnp.maximum(m_prev, jnp.max(st, axis=0, keepdims=True))
    alpha = jnp.exp(m_prev - m_new)
    p = jnp.exp(st - m_new)
    if keep is not None:
        p = jnp.where(keep, p, 0.0)
    l_ref[...] = alpha * l_ref[...] + jnp.sum(p, axis=0, keepdims=True)
    acc_ref[...] = alpha * acc_ref[...] + jnp.dot(vt, p.astype(BF16), preferred_element_type=F32)
    m_ref[...] = m_new


def _causal_t(tk, n, tq):
    si = lax.broadcasted_iota(I32, (tk, n), 0)
    ti = lax.broadcasted_iota(I32, (tk, n), 1) % tq
    return si <= ti


def _store_heads_t(o_ref, ot, tq):
    for h in range(N_HEADS):
        o_ref[h] = ot[:, h * tq:(h + 1) * tq].T.astype(BF16)


def _init_state(m_ref, l_ref, acc_ref):
    m_ref[...] = jnp.full(m_ref.shape, NEG_INF, F32)
    l_ref[...] = jnp.zeros(l_ref.shape, F32)
    acc_ref[...] = jnp.zeros(acc_ref.shape, F32)


def _split3(x):
    hi = x.astype(BF16).astype(F32)
    mid = (x - hi).astype(BF16).astype(F32)
    return hi, mid, x - hi - mid


def _fox_kernel(qt_ref, aux_ref, k_ref, vt_ref, o_ref, qs_ref, m_ref, l_ref, acc_ref, *, tq):
    i = pl.program_id(1)
    ri = lax.broadcasted_iota(I32, (LANES, tq), 0)
    for h in range(N_HEADS):
        qs_ref[0:LANES, h * tq:(h + 1) * tq] = qt_ref[h * LANES:(h + 1) * LANES, :]
        hi, mid, lo = _split3(aux_ref[h:h + 1, :])
        blk = jnp.where((ri < 3 * N_HEADS) & (ri % N_HEADS == h), -1.0, 0.0)
        blk = jnp.where(ri == 3 * N_HEADS, hi, blk)
        blk = jnp.where(ri == 3 * N_HEADS + 1, mid, blk)
        blk = jnp.where(ri == 3 * N_HEADS + 2, lo, blk)
        qs_ref[LANES:2 * LANES, h * tq:(h + 1) * tq] = blk.astype(BF16)
    _init_state(m_ref, l_ref, acc_ref)

    def step(j, diag):
        k = k_ref[pl.ds(pl.multiple_of(j * tq, tq), tq), :]
        st = jnp.dot(k, qs_ref[...], preferred_element_type=F32)
        if diag:
            st = jnp.where(_causal_t(tq, N_HEADS * tq, tq), st, NEG_INF)
        _col_online(st, vt_ref[j], m_ref, l_ref, acc_ref)

    def body(j, c):
        step(j, False)
        return c

    lax.fori_loop(0, i, body, 0)
    step(i, True)
    _store_heads_t(o_ref, acc_ref[...] / l_ref[...], tq)


def _mla_kernel(qt_ref, k_ref, vt_ref, o_ref, qs_ref, m_ref, l_ref, acc_ref, *, tq):
    i = pl.program_id(1)
    for h in range(N_HEADS):
        qs_ref[:, h * tq:(h + 1) * tq] = qt_ref[h * 2 * LANES:(h + 1) * 2 * LANES, :]
    _init_state(m_ref, l_ref, acc_ref)

    def step(j, diag):
        k = k_ref[pl.ds(pl.multiple_of(j * tq, tq), tq), :]
        st = jnp.dot(k, qs_ref[...], preferred_element_type=F32) * (MLA_NOPE + MLA_ROPE) ** -0.5
        if diag:
            st = jnp.where(_causal_t(tq, N_HEADS * tq, tq), st, NEG_INF)
        _col_online(st, vt_ref[j], m_ref, l_ref, acc_ref)

    def body(j, c):
        step(j, False)
        return c

    lax.fori_loop(0, i, body, 0)
    step(i, True)
    _store_heads_t(o_ref, acc_ref[...] / l_ref[...], tq)


def _near_far_loop(i, step):
    def body(j, c):
        step(j, 2)
        return c

    lax.fori_loop(0, jnp.maximum(i - 1, 0), body, 0)

    @pl.when(i >= 1)
    def _():
        step(i - 1, 1)

    step(i, 0)


def _diff_kernel(lam_ref, qt_ref, k_ref, vt_ref, tb_ref, tfar_ref, g_ref, o_ref, qs_ref, m_ref, l_ref, acc_ref,
                 *, tq, lam_init):
    i = pl.program_id(1)
    nq = N_HEADS * tq
    for g in range(2):
        for h in range(N_HEADS):
            qs_ref[:, (g * N_HEADS + h) * tq:(g * N_HEADS + h + 1) * tq] = \
                qt_ref[(2 * h + g) * LANES:(2 * h + g + 1) * LANES, :]
    _init_state(m_ref, l_ref, acc_ref)

    def step(j, dist):
        k = k_ref[pl.ds(pl.multiple_of(j * tq, tq), tq), :]
        st = jnp.dot(k, qs_ref[...], preferred_element_type=F32) * DIFF_HALF ** -0.5
        if dist == 2:
            st = st + tfar_ref[...]
        else:
            st = st + tb_ref[dist]
        if dist == 0:
            st = jnp.where(_causal_t(tq, 2 * nq, tq), st, NEG_INF)
        _col_online(st, vt_ref[j], m_ref, l_ref, acc_ref)

    _near_far_loop(i, step)
    ot = acc_ref[...] / l_ref[...]
    ot = ot[:, :nq] - lam_ref[0] * ot[:, nq:]
    ri = lax.broadcasted_iota(I32, ot.shape, 0)
    ot = jnp.where(ri >= HEAD_DIM, ot, 0.0)
    y = ot * lax.rsqrt(jnp.sum(ot * ot, axis=0, keepdims=True) * (1.0 / HEAD_DIM) + EPS) * g_ref[...]
    _store_heads_t(o_ref, y * (1.0 - lam_init), tq)


def _score_keys(sc):
    bits = lax.bitcast_convert_type(sc, I32)
    key = jnp.where(bits < 0, bits ^ jnp.int32(0x7FFFFFFF), bits)
    return jnp.where(sc == 0.0, 0, key)


def _dsa_kernel(qit_ref, iwt_ref, ki_ref, qt_ref, k_ref, vt_ref, tb_ref, tfar_ref, o_ref,
                keys_ref, qis_ref, wr_ref, thr_ref, qs_ref, m_ref, l_ref, acc_ref, *, tq, nsel):
    i = pl.program_id(1)
    for h in range(IDX_HEADS):
        qis_ref[:, h * tq:(h + 1) * tq] = qit_ref[h * LANES:(h + 1) * LANES, :]
        wr_ref[:, h * tq:(h + 1) * tq] = iwt_ref[h:h + 1, :] * IDX_HEADS ** -0.5
    for h in range(N_HEADS):
        qs_ref[:, h * tq:(h + 1) * tq] = qt_ref[h * LANES:(h + 1) * LANES, :]
    _init_state(m_ref, l_ref, acc_ref)

    def score_tile(j, diag):
        kt = ki_ref[pl.ds(pl.multiple_of(j * tq, tq), tq), :]
        r = jnp.maximum(jnp.dot(kt, qis_ref[...], preferred_element_type=F32) * IDX_DIM ** -0.5, 0.0) * wr_ref[...]
        sc = r[:, 0:tq]
        for h in range(1, IDX_HEADS):
            sc = sc + r[:, h * tq:(h + 1) * tq]
        key = _score_keys(sc)
        if diag:
            key = jnp.where(_causal_t(tq, tq, tq), key, INT_MIN)
        keys_ref[j] = key

    def sbody(j, c):
        score_tile(j, False)
        return c

    lax.fori_loop(0, i, sbody, 0)
    score_tile(i, True)

    def count_ge(cand):
        def cbody(j, c):
            ge = jnp.where(keys_ref[j] >= cand, 1.0, 0.0)
            return c + jnp.sum(ge.reshape(tq // 8, 8, tq), axis=0)

        c = lax.fori_loop(0, i + 1, cbody, jnp.zeros((8, tq), F32))
        return jnp.sum(c, axis=0, keepdims=True)

    zero = jnp.zeros((1, tq), I32)
    t0 = jnp.where(count_ge(zero) >= nsel, zero, INT_MIN)

    def bit_body(b, t):
        cand = t + jnp.left_shift(jnp.int32(1), 30 - b)
        return jnp.where(count_ge(cand) >= nsel, cand, t)

    thr_ref[...] = lax.fori_loop(0, 31, bit_body, t0)

    def step(j, dist):
        k = k_ref[pl.ds(pl.multiple_of(j * tq, tq), tq), :]
        st = jnp.dot(k, qs_ref[...], preferred_element_type=F32)
        if dist == 2:
            st = st + tfar_ref[...]
        else:
            st = st + tb_ref[dist]
        sel = keys_ref[j] >= thr_ref[...]
        if dist == 0:
            sel = sel & _causal_t(tq, tq, tq)
        self32 = jnp.where(sel, 1.0, 0.0)
        keep = jnp.concatenate([self32] * N_HEADS, axis=1) > 0.5
        st = jnp.where(keep, st, NEG_INF)
        _col_online(st, vt_ref[j], m_ref, l_ref, acc_ref, keep=keep)

    _near_far_loop(i, step)
    _store_heads_t(o_ref, acc_ref[...] / l_ref[...], tq)


def _prompt_attention(pr, t5_table, lam, lam_init, subln, bsz, seq):
    _, _, kvb, vt, qt, aux = pr
    tq = min(Q_TILE, seq)
    assert seq % tq == 0 and tq == min(ROW_TILE, bsz * seq)
    nt = seq // tq
    kvb3 = kvb.reshape(bsz, seq, kvb.shape[-1])
    vt4 = vt.reshape(bsz, nt, vt.shape[1], tq)

    qspec = lambda n, blk: pl.BlockSpec((None, n * LANES, tq), lambda b, i: (b * nt + i, blk, 0))
    auxspec = lambda blk: pl.BlockSpec((None, LANES, tq), lambda b, i: (b * nt + i, blk, 0))
    kspec = lambda n, blk: pl.BlockSpec((None, seq, n * LANES), lambda b, i: (b, 0, blk))
    vspec = lambda blk: pl.BlockSpec((None, nt, LANES, tq), lambda b, i: (b, 0, blk, 0))
    ospec = pl.BlockSpec((None, N_HEADS, tq, LANES), lambda b, i: (b, 0, i, 0))
    oshape = jax.ShapeDtypeStruct((bsz, N_HEADS, seq, LANES), BF16)
    rowv = lambda n: pltpu.VMEM((1, n), F32)
    grid = (bsz, nt)
    sem = ("arbitrary", "arbitrary")
    nq = N_HEADS * tq

    o_fox = pl.pallas_call(
        functools.partial(_fox_kernel, tq=tq), grid=grid,
        in_specs=[qspec(4, 6), auxspec(0), kspec(2, 1), vspec(1)],
        out_specs=ospec, out_shape=oshape,
        scratch_shapes=[pltpu.VMEM((2 * LANES, nq), BF16), rowv(nq), rowv(nq), pltpu.VMEM((LANES, nq), F32)],
        compiler_params=_cparams(sem), name="fox_attention",
    )(qt, aux, kvb3, vt4)

    o_mla = pl.pallas_call(
        functools.partial(_mla_kernel, tq=tq), grid=grid,
        in_specs=[qspec(8, 0), kspec(2, 0), vspec(0)],
        out_specs=ospec, out_shape=oshape,
        scratch_shapes=[pltpu.VMEM((2 * LANES, nq), BF16), rowv(nq), rowv(nq), pltpu.VMEM((LANES, nq), F32)],
        compiler_params=_cparams(sem), name="mla_attention",
    )(qt, kvb3, vt4)

    ri = jnp.arange(tq, dtype=I32)
    rel0 = ri[:, None] - ri[None, :]
    far_rel = jnp.full((1, 1), 2 * MAX_DISTANCE, I32)

    def t5_tiles(cols, reps):
        tb = jnp.stack([_t5_rows(cols, rel0, reps).T, _t5_rows(cols, rel0 + tq, reps).T], axis=0)
        tfar = jnp.repeat(_t5_rows(cols, far_rel, reps), tq, axis=0).T
        return tb.astype(F32), tfar.astype(F32)

    assert tq >= MAX_DISTANCE
    tb_d, tfar_d = t5_tiles(t5_table[:, :N_HEADS], 2)
    tb_s, tfar_s = t5_tiles(t5_table[:, N_HEADS:], 1)
    full = lambda a: pl.BlockSpec(a.shape, lambda b, i: (0,) * a.ndim)
    gcol = subln.T

    o_diff = pl.pallas_call(
        functools.partial(_diff_kernel, tq=tq, lam_init=lam_init), grid=grid,
        in_specs=[pl.BlockSpec(memory_space=pltpu.SMEM), qspec(8, 1), kspec(1, 4), vspec(2),
                  full(tb_d), full(tfar_d), full(gcol)],
        out_specs=ospec, out_shape=oshape,
        scratch_shapes=[pltpu.VMEM((LANES, 2 * nq), BF16), rowv(2 * nq), rowv(2 * nq),
                        pltpu.VMEM((LANES, 2 * nq), F32)],
        compiler_params=_cparams(sem), name="diff_attention",
    )(lam.reshape(1), qt, kvb3, vt4, tb_d, tfar_d, gcol)

    nsel = min(IDX_TOPK, seq // 4)
    o_dsa = pl.pallas_call(
        functools.partial(_dsa_kernel, tq=tq, nsel=nsel), grid=grid,
        in_specs=[qspec(8, 2), auxspec(1), kspec(1, 6), qspec(4, 7), kspec(1, 5), vspec(3),
                  full(tb_s), full(tfar_s)],
        out_specs=ospec, out_shape=oshape,
        scratch_shapes=[pltpu.VMEM((nt, tq, tq), I32), pltpu.VMEM((LANES, IDX_HEADS * tq), BF16),
                        rowv(IDX_HEADS * tq), pltpu.VMEM((1, tq), I32),
                        pltpu.VMEM((LANES, nq), BF16), rowv(nq), rowv(nq), pltpu.VMEM((LANES, nq), F32)],
        compiler_params=_cparams(sem), name="dsa_attention",
    )(qt, aux, kvb3, qt, kvb3, vt4, tb_s, tfar_s)
    return o_fox, o_mla, o_diff, o_dsa


def _out_kernel(x_ref, of_ref, om_ref, od_ref, os_ref, wo_ref, wv_ref, g1_ref, sc2_ref, sh2_ref,
                lng_ref, lnb_ref, rw_ref, rb_ref, x1_ref, h2_ref, ti_ref, tg_ref, rk_ref, cnt_ref, cnt_sc,
                *, tr, alpha):
    d = x_ref.shape[-1]
    acc = jnp.zeros((tr, d), F32)
    for h in range(N_HEADS):
        acc = acc + _bdot(of_ref[:, h].reshape(tr, LANES), wo_ref[h])
        o_mla = _bdot(om_ref[:, h].reshape(tr, LANES), wv_ref[h])
        acc = acc + _bdot(o_mla, wo_ref[N_HEADS + h])
        acc = acc + _bdot(od_ref[:, h].reshape(tr, LANES), wo_ref[2 * N_HEADS + h])
        acc = acc + _bdot(os_ref[:, h].reshape(tr, LANES), wo_ref[3 * N_HEADS + h])
    x1 = _layer_norm(alpha * x_ref[...] + g1_ref[0] * acc, lng_ref[...], lnb_ref[...])
    x1_ref[...] = x1
    h2 = x1 * (1.0 + sc2_ref[0]) + sh2_ref[0]
    h2_ref[...] = h2
    logits = jnp.dot(h2, rw_ref[...], precision=lax.Precision.HIGHEST, preferred_element_type=F32) + rb_ref[...]
    lane = lax.broadcasted_iota(I32, (tr, LANES), 1).astype(F32)
    idx_out = jnp.zeros((tr, LANES), F32)
    vals, hots = [], []
    for k in range(TOP_K):
        mx = jnp.max(logits, axis=1, keepdims=True)
        ix = jnp.min(jnp.where(logits == mx, lane, float(LANES)), axis=1, keepdims=True)
        idx_out = jnp.where(lane == k, ix, idx_out)
        vals.append(mx)
        hots.append(lane == ix)
        logits = jnp.where(hots[-1], -3.4e38, logits)
    @pl.when(pl.program_id(0) == 0)
    def _():
        cnt_sc[...] = jnp.zeros_like(cnt_sc)
    picked = sum(jnp.where(hh, 1.0, 0.0) for hh in hots)
    ti_ = lax.broadcasted_iota(I32, (tr, tr), 0)
    tj_ = lax.broadcasted_iota(I32, (tr, tr), 1)
    before = _bdot(jnp.where(tj_ < ti_, 1.0, 0.0), picked) + cnt_sc[0:1, :]
    rank_out = jnp.zeros((tr, LANES), F32)
    for k in range(TOP_K):
        rk = jnp.sum(jnp.where(hots[k], before, 0.0), axis=1, keepdims=True)
        rank_out = jnp.where(lane == k, rk, rank_out)
    rk_ref[...] = rank_out.astype(I32)
    cnt_new = cnt_sc[...] + jnp.sum(picked, axis=0, keepdims=True)
    cnt_sc[...] = cnt_new
    cnt_ref[...] = cnt_new
    es = [jnp.exp(v - vals[0]) for v in vals]
    den = es[0] + es[1] + es[2] + es[3]
    gates = jnp.zeros((tr, LANES), F32)
    for k in range(TOP_K):
        gates = jnp.where(lane == k, es[k] / den, gates)
    ti_ref[...] = idx_out.astype(I32)
    tg_ref[...] = gates


def _out_router(x2d, o_list, wo, wv, gate1, scale2, shift2, ln_g, ln_b, rw, rb, rows_per_mod, alpha):
    rtot, d = x2d.shape
    tr = min(ROW_TILE, rtot)
    nb_o, _, tt, _ = o_list[0].shape
    gate1, scale2, shift2 = _tile_mod(gate1, tr), _tile_mod(scale2, tr), _tile_mod(shift2, tr)
    mr = gate1.shape[1]
    mod_map = _mod_map(mr, tr, rows_per_mod)
    if tt >= tr:
        assert tt % tr == 0
        per = tt // tr
        ospec = pl.BlockSpec((1, N_HEADS, tr, LANES), lambda r: (r // per, 0, r % per, 0))
    else:
        assert tr % tt == 0
        ospec = pl.BlockSpec((tr // tt, N_HEADS, tt, LANES), lambda r: (r, 0, 0, 0))
    full = lambda a: pl.BlockSpec(a.shape, lambda r: (0,) * a.ndim)
    row = lambda n: pl.BlockSpec((tr, n), lambda r: (r, 0))
    mspec = pl.BlockSpec((1, mr, d), mod_map)
    return pl.pallas_call(
        functools.partial(_out_kernel, tr=tr, alpha=alpha),
        grid=(rtot // tr,),
        in_specs=[row(d), ospec, ospec, ospec, ospec, full(wo), full(wv), mspec, mspec, mspec,
                  full(ln_g), full(ln_b), full(rw), full(rb)],
        out_specs=[row(d), row(d), row(LANES), row(LANES), row(LANES), pl.BlockSpec((8, LANES), lambda r: (0, 0))],
        out_shape=[jax.ShapeDtypeStruct((rtot, d), F32), jax.ShapeDtypeStruct((rtot, d), F32),
                   jax.ShapeDtypeStruct((rtot, LANES), I32), jax.ShapeDtypeStruct((rtot, LANES), F32),
                   jax.ShapeDtypeStruct((rtot, LANES), I32), jax.ShapeDtypeStruct((8, LANES), F32)],
        scratch_shapes=[pltpu.VMEM((8, LANES), F32)],
        compiler_params=_cparams(("arbitrary",)),
        name="out_proj_router",
    )(x2d, *o_list, wo, wv, gate1, scale2, shift2, ln_g, ln_b, rw, rb)


def _routing_tables(top_idx, rank, counts, n_tok):
    a = n_tok * TOP_K
    e_flat = top_idx.reshape(a)
    padded = (counts + EXPERT_BLOCK - 1) // EXPERT_BLOCK * EXPERT_BLOCK
    pstart = jnp.cumsum(padded) - padded
    dest = (pstart[e_flat] + rank.reshape(a)).astype(I32)
    n_blocks = -(-a // EXPERT_BLOCK) + N_EXPERTS
    src = jnp.zeros((n_blocks * EXPERT_BLOCK,), I32).at[dest].set(jnp.arange(a, dtype=I32) // TOP_K)
    blk_expert = jnp.minimum(
        jnp.searchsorted(pstart + padded, jnp.arange(n_blocks, dtype=I32) * EXPERT_BLOCK, side='right'),
        N_EXPERTS - 1).astype(I32)
    return dest, src, blk_expert, n_blocks


def _moe_kernel(be_ref, src_ref, h_hbm, wgu_ref, bgu_ref, wdn_ref, bdn_ref, y_ref,
                xbuf0, xbuf1, sem, wgu_b, wdn_b, *, nblk, dff):
    i = pl.program_id(0)
    slot = i % 2

    bufs = (xbuf0, xbuf1)

    def row_copy(blk, sl, r):
        tok = src_ref[blk * EXPERT_BLOCK + r]
        return pltpu.make_async_copy(h_hbm.at[pl.ds(tok, 1)], bufs[sl].at[pl.ds(r, 1)], sem.at[sl])

    def wait_rows(blk, sl):
        def wbody(r, c):
            row_copy(blk, sl, r).wait()
            return c
        lax.fori_loop(0, EXPERT_BLOCK, wbody, 0)

    @pl.when(i == 0)
    def _():
        def body(r, c):
            row_copy(0, 0, r).start()
            return c
        lax.fori_loop(0, EXPERT_BLOCK, body, 0)

    prev = be_ref[jnp.maximum(i - 1, 0)]

    @pl.when((i == 0) | (be_ref[i] != prev))
    def _():
        wgu_b[...] = wgu_ref[0, 0].astype(BF16)
        wdn_b[...] = wdn_ref[0, 0].astype(BF16)

    nxt = jnp.minimum(i + 1, nblk - 1)
    for par in range(2):
        @pl.when(slot == par)
        def _(par=par):
            wait_rows(i, par)
            for r in range(EXPERT_BLOCK):
                row_copy(nxt, 1 - par, r).start()
            gu = jnp.dot(bufs[par][...].astype(BF16), wgu_b[...], preferred_element_type=F32) + bgu_ref[0, 0]
            g = jnp.minimum(gu[:, :dff], SWIGLU_LIMIT)
            u = jnp.clip(gu[:, dff:], -SWIGLU_LIMIT, SWIGLU_LIMIT)
            act = g * jax.nn.sigmoid(SWIGLU_ALPHA * g) * (u + 1.0)
            y_ref[...] = jnp.dot(act.astype(BF16), wdn_b[...], preferred_element_type=F32) + bdn_ref[0, 0]

            @pl.when(i == nblk - 1)
            def _():
                wait_rows(nxt, 1 - par)


def _moe_blocks(l, h2, blk_expert, src, n_blocks, exp_w_gu, exp_b_gu, exp_w_dn, exp_b_dn):
    rtot, d = h2.shape
    dff = exp_w_dn.shape[2]
    bgu = exp_b_gu.reshape(exp_b_gu.shape[0], N_EXPERTS, 1, 2 * dff)
    bdn = exp_b_dn.reshape(exp_b_dn.shape[0], N_EXPERTS, 1, d)
    gs = pltpu.PrefetchScalarGridSpec(
        num_scalar_prefetch=2,
        grid=(n_blocks,),
        in_specs=[pl.BlockSpec(memory_space=pl.ANY),
                  pl.BlockSpec((1, 1, d, 2 * dff), lambda i, be, s: (l, be[i], 0, 0)),
                  pl.BlockSpec((1, 1, 1, 2 * dff), lambda i, be, s: (l, be[i], 0, 0)),
                  pl.BlockSpec((1, 1, dff, d), lambda i, be, s: (l, be[i], 0, 0)),
                  pl.BlockSpec((1, 1, 1, d), lambda i, be, s: (l, be[i], 0, 0))],
        out_specs=pl.BlockSpec((EXPERT_BLOCK, d), lambda i, be, s: (i, 0)),
        scratch_shapes=[pltpu.VMEM((EXPERT_BLOCK, d), F32), pltpu.VMEM((EXPERT_BLOCK, d), F32),
                        pltpu.SemaphoreType.DMA((2,)),
                        pltpu.VMEM((d, 2 * dff), BF16), pltpu.VMEM((dff, d), BF16)])
    return pl.pallas_call(
        functools.partial(_moe_kernel, nblk=n_blocks, dff=dff),
        grid_spec=gs,
        out_shape=jax.ShapeDtypeStruct((n_blocks * EXPERT_BLOCK, d), F32),
        compiler_params=_cparams(("arbitrary",)),
        name="moe_experts",
    )(blk_expert, src, h2, exp_w_gu, bgu, exp_w_dn, bdn)


def _final_kernel(dest_ref, x1_ref, y_hbm, tg_ref, g2_ref, lng_ref, lnb_ref, o_ref, ybuf, sem,
                  *, nsteps, tr, alpha):
    i = pl.program_id(0)
    slot = i % 2

    def row_copy(step, sl, r, k):
        pos = dest_ref[(step * tr + r) * TOP_K + k]
        return pltpu.make_async_copy(y_hbm.at[pl.ds(pos, 1)], ybuf.at[sl, k, pl.ds(r, 1)], sem.at[sl])

    def issue(step, sl):
        def body(r, c):
            for k in range(TOP_K):
                row_copy(step, sl, r, k).start()
            return c
        lax.fori_loop(0, tr, body, 0)

    @pl.when(i == 0)
    def _():
        issue(0, 0)

    @pl.when(i + 1 < nsteps)
    def _():
        issue(i + 1, 1 - slot)

    def wbody(r, c):
        for k in range(TOP_K):
            row_copy(i, slot, r, k).wait()
        return c
    lax.fori_loop(0, tr, wbody, 0)

    tg = tg_ref[...]
    y = tg[:, 0:1] * ybuf[slot, 0]
    for k in range(1, TOP_K):
        y = y + tg[:, k:k + 1] * ybuf[slot, k]
    o_ref[...] = _layer_norm(alpha * x1_ref[...] + g2_ref[0] * y, lng_ref[...], lnb_ref[...])


def _combine_final(x1, y_pad, dest, gates, gate2, ln_g, ln_b, rows_per_mod, alpha):
    rtot, d = x1.shape
    tr = min(FINAL_TILE, rtot)
    nsteps = rtot // tr
    gate2 = _tile_mod(gate2, tr)
    mod_map = _mod_map(gate2.shape[1], tr, rows_per_mod)
    gs = pltpu.PrefetchScalarGridSpec(
        num_scalar_prefetch=1,
        grid=(nsteps,),
        in_specs=[pl.BlockSpec((tr, d), lambda r, dst: (r, 0)),
                  pl.BlockSpec(memory_space=pl.ANY),
                  pl.BlockSpec((tr, LANES), lambda r, dst: (r, 0)),
                  pl.BlockSpec((1, gate2.shape[1], d), mod_map),
                  pl.BlockSpec(ln_g.shape, lambda r, dst: (0, 0)),
                  pl.BlockSpec(ln_b.shape, lambda r, dst: (0, 0))],
        out_specs=pl.BlockSpec((tr, d), lambda r, dst: (r, 0)),
        scratch_shapes=[pltpu.VMEM((2, TOP_K, tr, d), F32), pltpu.SemaphoreType.DMA((2,))])
    return pl.pallas_call(
        functools.partial(_final_kernel, nsteps=nsteps, tr=tr, alpha=alpha),
        grid_spec=gs,
        out_shape=jax.ShapeDtypeStruct((rtot, d), F32),
        compiler_params=_cparams(("arbitrary",)),
        name="moe_combine_norm",
    )(dest, x1, y_pad, gates, gate2, ln_g, ln_b)


def _chunk_copies(specs, sem, layer, pt_ref, b, chunk, slot, pages):
    cps = []
    for ci, (cache, buf, placement) in enumerate(specs):
        for p in range(pages):
            page = pt_ref[b, chunk * pages + p]
            src = cache.at[layer, page]
            if placement == 'rows':
                dst = buf.at[slot, pl.ds(p * LANES, LANES)]
            elif placement == 'page':
                dst = buf.at[slot, :, p, :]
            else:
                idx = (slot,) + (slice(None),) * (len(buf.shape) - 2) + (pl.ds(p * LANES, LANES),)
                dst = buf.at[idx]
            cps.append(pltpu.make_async_copy(src, dst, sem.at[slot, ci]))
    return cps


def _pipeline_fetch(specs, sem, layer, pt_ref, nch, pages, chunk_of):
    b, c = pl.program_id(0), pl.program_id(1)
    nb = pl.num_programs(0)
    g = b * nch + c
    slot = g % 2

    @pl.when(g == 0)
    def _():
        for cp in _chunk_copies(specs, sem, layer, pt_ref, b, chunk_of(c), slot, pages):
            cp.start()

    @pl.when(g + 1 < nb * nch)
    def _():
        g1 = g + 1
        b1, c1 = g1 // nch, g1 % nch
        for cp in _chunk_copies(specs, sem, layer, pt_ref, b1, chunk_of(c1), 1 - slot, pages):
            cp.start()

    for cp in _chunk_copies(specs, sem, layer, pt_ref, b, chunk_of(c), slot, pages):
        cp.wait()
    return slot


def _stack_slabs(x, n, width=1):
    return jnp.concatenate([x[:, h * width * LANES:(h + 1) * width * LANES] for h in range(n)], axis=0)


def _dec_idx_kernel(pt_ref, kidx_hbm, logf_hbm, qi_ref, iw_ref, knew_ref, kp_ref, kn_ref, thr_ref, fb_ref,
                    kbuf, lbuf, sem, allk, newk, *, layer, nch, pages, dec, nsel):
    c = pl.program_id(1)
    slot = _pipeline_fetch([(kidx_hbm, kbuf, 'lanes'), (logf_hbm, lbuf, 'page')], sem, layer, pt_ref, nch, pages,
                           lambda cc: cc)
    pj = lax.broadcasted_iota(I32, (LANES, LANES), 0)
    ps = lax.broadcasted_iota(I32, (LANES, LANES), 1)
    within = jnp.where(pj >= ps, 1.0, 0.0)
    gi = lax.broadcasted_iota(I32, (pages, pages), 0)
    gj = lax.broadcasted_iota(I32, (pages, pages), 1)
    later = jnp.where(gj > gi, 1.0, 0.0)
    hp = lax.Precision.HIGHEST
    for h in range(N_HEADS):
        x = lbuf[slot, h]
        incl = jnp.dot(x, within, precision=hp, preferred_element_type=F32)
        tot = jnp.broadcast_to(incl[:, 0:1], (pages, LANES))
        fb_ref[h] = incl - x + jnp.dot(later, tot, precision=hp, preferred_element_type=F32)

    q = _stack_slabs(qi_ref[...], IDX_HEADS)[:, :IDX_DIM]
    iw = iw_ref[...]
    wcol = jnp.concatenate([iw[:, h:h + 1] for h in range(IDX_HEADS)], axis=0) * IDX_HEADS ** -0.5

    def scores(kt):
        r = jnp.maximum(_bdot(q, kt) * IDX_DIM ** -0.5, 0.0) * wcol
        sc = r[0:dec]
        for h in range(1, IDX_HEADS):
            sc = sc + r[h * dec:(h + 1) * dec]
        return _score_keys(sc)

    key = scores(kbuf[slot])
    kp_ref[...] = key
    allk[c] = key

    @pl.when(c == nch - 1)
    def _():
        kn = scores(knew_ref[...])
        t = lax.broadcasted_iota(I32, kn.shape, 0)
        s = lax.broadcasted_iota(I32, kn.shape, 1)
        kn = jnp.where(s <= t, kn, INT_MIN)
        kn_ref[...] = kn
        newk[...] = kn

        def count_ge(cand):
            def cbody(j, acc):
                ge = jnp.where(allk[j] >= cand, 1.0, 0.0)
                parts = [ge[:, i * LANES:(i + 1) * LANES] for i in range(ge.shape[1] // LANES)]
                while len(parts) > 1:
                    parts = [a + b for a, b in zip(parts[0::2], parts[1::2])] + (parts[-1:] if len(parts) % 2 else [])
                return acc + parts[0]
            acc = lax.fori_loop(0, nch, cbody, jnp.where(newk[...] >= cand, 1.0, 0.0))
            return jnp.sum(acc, axis=1, keepdims=True)

        zero = jnp.zeros((dec, 1), I32)
        t0 = jnp.where(count_ge(zero) >= nsel, zero, INT_MIN)

        def bit_body(bi, tcur):
            cand = tcur + jnp.left_shift(jnp.int32(1), 30 - bi)
            return jnp.where(count_ge(cand) >= nsel, cand, tcur)

        thr = lax.fori_loop(0, 31, bit_body, t0)
        thr_ref[...] = jnp.broadcast_to(thr, thr_ref.shape)


def _suffix_sum(x):
    n = x.shape[1]
    lane = lax.broadcasted_iota(I32, x.shape, 1)
    d = 1
    while d < n:
        x = x + jnp.where(lane + d < n, pltpu.roll(x, n - d, 1), 0.0)
        d *= 2
    return x


def _online_update_fn(s, pv, m_ref, l_ref, acc_ref, keep=None):
    m_prev = m_ref[...]
    m_new = jnp.maximum(m_prev, jnp.max(s, axis=1, keepdims=True))
    alpha = jnp.exp(m_prev - m_new)
    p = jnp.exp(s - m_new)
    if keep is not None:
        p = jnp.where(keep, p, 0.0)
    l_ref[...] = alpha * l_ref[...] + jnp.sum(p, axis=1, keepdims=True)
    acc_ref[...] = alpha * acc_ref[...] + pv(p)
    m_ref[...] = m_new


def _dec_att_kernel(pt_ref, lam_ref,
                    fox_hbm, ckv_hbm, kr_hbm, diff_hbm, dsa_hbm,
                    qf_ref, qm_ref, qd_ref, qs_ref, ecol_ref, fb_ref, kp_ref, kn_ref, thr_ref,
                    nf_ref, nl_ref, nc_ref, nk_ref, nd_ref, ns_ref,
                    tbd_ref, tbs_ref, tnd_ref, tns_ref, g_ref,
                    of_ref, om_ref, od_ref, os_ref,
                    fbuf, cbuf, kbuf, dbuf, sbuf, sem,
                    mf, lf_, af, mm, lm, am, md, ld, ad, ms, ls, as_, rcar,
                    *, layer, nch, pages, dec, lam_init):
    c = pl.program_id(1)
    specs = [(fox_hbm, fbuf, 'lanes'), (ckv_hbm, cbuf, 'rows'), (kr_hbm, kbuf, 'lanes'),
             (diff_hbm, dbuf, 'lanes'), (dsa_hbm, sbuf, 'lanes')]
    slot = _pipeline_fetch(specs, sem, layer, pt_ref, nch, pages, lambda cc: nch - 1 - cc)
    nr = N_HEADS * dec

    qf = _stack_slabs(qf_ref[...], N_HEADS)[:, :HEAD_DIM]
    qm = _stack_slabs(qm_ref[...], N_HEADS, 2)
    qlat, qrope = qm[:, :LANES], qm[:, LANES:LANES + MLA_ROPE]
    qd_all = qd_ref[...]
    qd = jnp.concatenate([qd_all[:, (2 * h + g) * LANES:(2 * h + g + 1) * LANES]
                          for g in range(2) for h in range(N_HEADS)], axis=0)[:, :HEAD_DIM]
    qs = _stack_slabs(qs_ref[...], N_HEADS)[:, :HEAD_DIM]
    ecol = ecol_ref[...]
    thr = thr_ref[:, 0:1]

    def process(fkv, after, ckv, krt, dkv, skv, keys, t5d, t5s, is_new):
        nk = after.shape[1]
        if is_new:
            trow = lax.broadcasted_iota(I32, (dec, nk), 0)
            scol = lax.broadcasted_iota(I32, (dec, nk), 1)
            ok8 = jnp.where(scol <= trow, 1.0, 0.0)
            ok = jnp.concatenate([ok8] * N_HEADS, axis=0) > 0.5
            ok2 = jnp.concatenate([ok8] * (2 * N_HEADS), axis=0) > 0.5
        s = _bdot(qf, fkv[0]) * HEAD_DIM ** -0.5
        excl = after + rcar[...]
        bias = jnp.concatenate([jnp.broadcast_to(excl[h:h + 1, :], (dec, nk)) for h in range(N_HEADS)], axis=0)
        s = s + bias - ecol
        if is_new:
            s = jnp.where(ok, s, NEG_INF)
        _online_update_fn(s, lambda p: _bdot_nt(p, fkv[1]), mf, lf_, af)
        s = (_bdot_nt(qlat, ckv) + _bdot(qrope, krt)) * (MLA_NOPE + MLA_ROPE) ** -0.5
        if is_new:
            s = jnp.where(ok, s, NEG_INF)
        _online_update_fn(s, lambda p: _bdot(p, ckv), mm, lm, am)
        s = _bdot(qd, dkv[0]) * DIFF_HALF ** -0.5 + t5d
        if is_new:
            s = jnp.where(ok2, s, NEG_INF)
        _online_update_fn(s, lambda p: _bdot_nt(p, dkv[1]), md, ld, ad)
        s = _bdot(qs, skv[0]) * HEAD_DIM ** -0.5 + t5s
        sel8 = jnp.where(keys >= thr, 1.0, 0.0)
        if is_new:
            sel8 = sel8 * ok8
        keep = jnp.concatenate([sel8] * N_HEADS, axis=0) > 0.5
        s = jnp.where(keep, s, NEG_INF)
        _online_update_fn(s, lambda p: _bdot_nt(p, skv[1]), ms, ls, as_, keep=keep)

    @pl.when(c == 0)
    def _():
        for m_, l_, a_ in ((mf, lf_, af), (mm, lm, am), (md, ld, ad), (ms, ls, as_)):
            _init_state(m_, l_, a_)
        rcar[...] = jnp.zeros_like(rcar)
        lgf_new = nl_ref[...]
        incl_new = _suffix_sum(lgf_new)
        process(nf_ref[...], incl_new - lgf_new, nc_ref[...], nk_ref[...], nd_ref[...], ns_ref[...], kn_ref[...],
                tnd_ref[...], tns_ref[...], True)
        rcar[...] = incl_new[:, 0:1]

    process(fbuf[slot], fb_ref[...], cbuf[slot], kbuf[slot], dbuf[slot], sbuf[slot], kp_ref[...],
            tbd_ref[...], tbs_ref[...], False)

    @pl.when(c == nch - 1)
    def _():
        def put(o_ref, o):
            w = o.shape[1]
            for h in range(N_HEADS):
                if w < LANES:
                    o_ref[h] = jnp.zeros((dec, LANES), BF16)
                    o_ref[h, :, 0:w] = o[h * dec:(h + 1) * dec].astype(BF16)
                else:
                    o_ref[h] = o[h * dec:(h + 1) * dec].astype(BF16)

        put(of_ref, af[...] / lf_[...])
        put(om_ref, am[...] / lm[...])
        od = ad[...] / ld[...]
        od = od[:nr] - lam_ref[0] * od[nr:]
        y = od * lax.rsqrt(jnp.mean(od * od, axis=-1, keepdims=True) + EPS) * g_ref[...] * (1.0 - lam_init)
        put(od_ref, y)
        put(os_ref, as_[...] / ls[...])


def _split_rows(rows, lead):
    s = lambda i, n: rows[:, i * LANES:i * LANES + n]
    return {
        'fox_kv': s(0, 2 * HEAD_DIM).reshape(lead + (2, HEAD_DIM)),
        'fox_logf': s(1, N_HEADS).reshape(lead + (N_HEADS,)),
        'mla_ckv': s(2, MLA_KV_RANK).reshape(lead + (MLA_KV_RANK,)),
        'mla_krope': s(3, MLA_ROPE).reshape(lead + (MLA_ROPE,)),
        'diff_kv': s(4, 2 * HEAD_DIM).reshape(lead + (2, HEAD_DIM)),
        'dsa_kv': s(5, 2 * HEAD_DIM).reshape(lead + (2, HEAD_DIM)),
        'dsa_kidx': s(6, IDX_DIM).reshape(lead + (IDX_DIM,)),
    }


def _ffn_and_norm(l, routed, gate2, ln_g2, ln_b2, ew, rows_per_mod, alpha):
    x1, h2, ti, tg, rk, cnt = routed
    n_tok = x1.shape[0]
    counts = cnt[0, :N_EXPERTS].astype(I32)
    dest, src, blk_expert, n_blocks = _routing_tables(ti[:, :TOP_K], rk[:, :TOP_K], counts, n_tok)
    y_pad = _moe_blocks(l, h2, blk_expert, src, n_blocks, *ew)
    return _combine_final(x1, y_pad, dest, tg, gate2, ln_g2, ln_b2, rows_per_mod, alpha)


def _lambda(l, diff_lambda):
    lam_init = 0.8 - 0.6 * math.exp(-0.3 * l)
    dl = diff_lambda[l]
    lam = (jnp.exp(jnp.sum(dl[0] * dl[1]).astype(F32)) - jnp.exp(jnp.sum(dl[2] * dl[3]).astype(F32)) + lam_init)
    return lam, lam_init


def _decode_attention(l, pr, caches_t, page_table, t5_table, lam, lam_init, subln_lo, bd, dec):
    rows, cum, qf, qm, qd, qs, qi, iw = pr
    fox_t, logf_t, ckv_c, kr_t, diff_t, dsa_t, kidx_t = caches_t
    n_pages = page_table.shape[1]
    pages = min(PAGES_PER_STEP, n_pages)
    assert n_pages % pages == 0
    nch = n_pages // pages
    gk = pages * LANES
    nk_past = n_pages * LANES
    nsel = min(IDX_TOPK, (nk_past + dec) // 4)

    rows3 = rows.reshape(bd, dec, rows.shape[-1])

    def key_minor(slab, n):
        x = jnp.transpose(rows3[:, :, slab * LANES:slab * LANES + n], (0, 2, 1))
        return _pad_last(x, LANES)

    nf = key_minor(0, 2 * HEAD_DIM).reshape(bd, 2, HEAD_DIM, LANES)
    nl = key_minor(1, N_HEADS)
    nc = jnp.pad(rows3[:, :, 2 * LANES:3 * LANES], ((0, 0), (0, LANES - dec), (0, 0)))
    nkr = key_minor(3, MLA_ROPE)
    nd = key_minor(4, 2 * HEAD_DIM).reshape(bd, 2, HEAD_DIM, LANES)
    ns = key_minor(5, 2 * HEAD_DIM).reshape(bd, 2, HEAD_DIM, LANES)
    nidx = key_minor(6, IDX_DIM)
    cum3 = cum.reshape(bd, dec, LANES)[:, :, :N_HEADS]
    ecol = jnp.transpose(cum3[:, dec - 1:dec, :] - cum3, (0, 2, 1)).reshape(bd, N_HEADS * dec, 1)

    sem = ("arbitrary", "arbitrary")
    row_blk = lambda n: pl.BlockSpec((dec, n), lambda b, c, *_: (b, 0))
    per_b = lambda a: pl.BlockSpec((None,) + a.shape[1:], lambda b, c, *_: (b,) + (0,) * (a.ndim - 1))
    anyspec = pl.BlockSpec(memory_space=pl.ANY)

    gs_idx = pltpu.PrefetchScalarGridSpec(
        num_scalar_prefetch=1, grid=(bd, 1),
        in_specs=[anyspec, anyspec, row_blk(8 * LANES), row_blk(LANES), per_b(nidx)],
        out_specs=[pl.BlockSpec((None, dec, nk_past), lambda b, c, pt: (b, 0, 0)),
                   pl.BlockSpec((None, dec, LANES), lambda b, c, pt: (b, 0, 0)),
                   pl.BlockSpec((None, dec, LANES), lambda b, c, pt: (b, 0, 0)),
                   pl.BlockSpec((None, N_HEADS, n_pages, LANES), lambda b, c, pt: (b, 0, 0, 0))],
        scratch_shapes=[pltpu.VMEM((2, IDX_DIM, nk_past), F32), pltpu.VMEM((2, N_HEADS, n_pages, LANES), F32),
                        pltpu.SemaphoreType.DMA((2, 2)),
                        pltpu.VMEM((1, dec, nk_past), I32), pltpu.VMEM((dec, LANES), I32)])
    keys_past, keys_new, thr, after = pl.pallas_call(
        functools.partial(_dec_idx_kernel, layer=l, nch=1, pages=n_pages, dec=dec, nsel=nsel),
        grid_spec=gs_idx,
        out_shape=[jax.ShapeDtypeStruct((bd, dec, nk_past), I32), jax.ShapeDtypeStruct((bd, dec, LANES), I32),
                   jax.ShapeDtypeStruct((bd, dec, LANES), I32),
                   jax.ShapeDtypeStruct((bd, N_HEADS, n_pages, LANES), F32)],
        compiler_params=_cparams(sem), name="decode_indexer",
    )(page_table, kidx_t, logf_t, qi, iw, nidx)
    after = after.reshape(bd, N_HEADS, nk_past)

    tpos = jnp.arange(dec, dtype=I32)[:, None]
    rel_last = gk + tpos - jnp.arange(gk, dtype=I32)[None, :]
    rel_far = jnp.full((dec, gk), 2 * MAX_DISTANCE, I32)
    rel_new = tpos - jnp.arange(LANES, dtype=I32)[None, :]

    def tiles(cols, reps):
        tb = jnp.stack([_t5_rows(cols, rel_far, reps), _t5_rows(cols, rel_last, reps)], axis=0).astype(F32)
        return tb, _t5_rows(cols, rel_new, reps).astype(F32)

    tbd, tnd = tiles(t5_table[:, :N_HEADS], 2)
    tbs, tns = tiles(t5_table[:, N_HEADS:], 1)
    recent = lambda a: pl.BlockSpec((None,) + a.shape[1:], lambda b, c, *_: (jnp.where(c == 0, 1, 0), 0, 0))
    full = lambda a: pl.BlockSpec(a.shape, lambda b, c, *_: (0,) * a.ndim)
    nr = N_HEADS * dec
    col = lambda n: pltpu.VMEM((n, 1), F32)
    ospec = pl.BlockSpec((None, N_HEADS, dec, LANES), lambda b, c, *_: (b, 0, 0, 0))
    oshape = jax.ShapeDtypeStruct((bd, N_HEADS, dec, LANES), BF16)
    gs_att = pltpu.PrefetchScalarGridSpec(
        num_scalar_prefetch=1, grid=(bd, nch),
        in_specs=[pl.BlockSpec(memory_space=pltpu.SMEM)] + [anyspec] * 5 + [
            row_blk(4 * LANES), row_blk(8 * LANES), row_blk(8 * LANES), row_blk(4 * LANES), per_b(ecol),
            pl.BlockSpec((None, N_HEADS, gk), lambda b, c, pt: (b, 0, nch - 1 - c)),
            pl.BlockSpec((None, dec, gk), lambda b, c, pt: (b, 0, nch - 1 - c)), per_b(keys_new), per_b(thr),
            per_b(nf), per_b(nl), per_b(nc), per_b(nkr), per_b(nd), per_b(ns),
            recent(tbd), recent(tbs), full(tnd), full(tns), full(subln_lo)],
        out_specs=[ospec] * 4,
        scratch_shapes=[pltpu.VMEM((2, 2, HEAD_DIM, gk), F32),
                        pltpu.VMEM((2, gk, LANES), F32), pltpu.VMEM((2, MLA_ROPE, gk), F32),
                        pltpu.VMEM((2, 2, HEAD_DIM, gk), F32), pltpu.VMEM((2, 2, HEAD_DIM, gk), F32),
                        pltpu.SemaphoreType.DMA((2, 5)),
                        col(nr), col(nr), pltpu.VMEM((nr, HEAD_DIM), F32),
                        col(nr), col(nr), pltpu.VMEM((nr, LANES), F32),
                        col(2 * nr), col(2 * nr), pltpu.VMEM((2 * nr, HEAD_DIM), F32),
                        col(nr), col(nr), pltpu.VMEM((nr, HEAD_DIM), F32),
                        pltpu.VMEM((N_HEADS, 1), F32)])
    return pl.pallas_call(
        functools.partial(_dec_att_kernel, layer=l, nch=nch, pages=pages, dec=dec, lam_init=lam_init),
        grid_spec=gs_att,
        out_shape=[oshape] * 4,
        compiler_params=_cparams(sem), name="decode_attention",
    )(page_table, lam.reshape(1), fox_t, ckv_c, kr_t, diff_t, dsa_t,
      qf, qm, qd, qs, ecol, after, keys_past, keys_new, thr, nf, nl, nc, nkr, nd, ns, tbd, tbs, tnd, tns, subln_lo)


def _sample_layer(l, x, mod, wts, w_out_l, ln_g, ln_b, ew, t5_table, diff_lambda, diff_subln, alpha,
                  caches_t, page_table):
    bd, dec, d = x.shape
    past_len = page_table.shape[1] * LANES
    mod_rows = jnp.repeat(mod, dec, axis=0)
    m = [mod_rows[:, k * d:(k + 1) * d] for k in range(6)]
    cos_t, sin_t = _rope_tables(past_len + jnp.arange(dec, dtype=I32))
    cos_t, sin_t = jnp.tile(cos_t, (bd, 1)), jnp.tile(sin_t, (bd, 1))
    x2d = x.reshape(bd * dec, d)
    pr = _project(x2d, m[1], m[0], wts, cos_t, sin_t, dec, 1, False)
    lam, lam_init = _lambda(l, diff_lambda)
    o_list = _decode_attention(l, pr, caches_t, page_table, t5_table, lam, lam_init, diff_subln[l][None, :], bd, dec)
    wo = _prep_w_out(w_out_l, False)
    routed = _out_router(x2d, o_list, wo, wts['wv'], m[2], m[4], m[3], ln_g[l, 0:1], ln_b[l, 0:1],
                         wts['rw'], wts['rb'], 1, alpha)
    x2 = _ffn_and_norm(l, routed, m[5], ln_g[l, 1:2], ln_b[l, 1:2], ew, 1, alpha)
    return x2.reshape(bd, dec, d), _split_rows(pr[0], (bd, dec))


def _prompt_layer(l, x, mod, wts, w_out_l, ln_g, ln_b, ew, t5_table, diff_lambda, alpha):
    bsz, seq, d = x.shape
    m = [mod[:, None, k * d:(k + 1) * d] for k in range(6)]
    cos_t, sin_t = _rope_tables(jnp.arange(seq, dtype=I32))
    cos_t, sin_t = jnp.tile(cos_t, (bsz, 1)), jnp.tile(sin_t, (bsz, 1))
    x2d = x.reshape(bsz * seq, d)
    pr = _project(x2d, m[1], m[0], wts, cos_t, sin_t, seq, seq, True)
    lam, lam_init = _lambda(l, diff_lambda)
    o_list = _prompt_attention(pr, t5_table, lam, lam_init, wts['subln'], bsz, seq)
    wo = _prep_w_out(w_out_l, True)
    routed = _out_router(x2d, o_list, wo, wts['wv'], m[2], m[4], m[3], ln_g[l, 0:1], ln_b[l, 0:1],
                         wts['rw'], wts['rb'], seq, alpha)
    x2 = _ffn_and_norm(l, routed, m[5], ln_g[l, 1:2], ln_b[l, 1:2], ew, seq, alpha)
    return x2.reshape(bsz, seq, d), _split_rows(pr[0], (bsz, seq))


def kernel(x_prompt, x_sample, c_prompt, c_sample, cache_fox_kv, cache_fox_logf, cache_mla_ckv, cache_mla_krope,
           cache_diff_kv, cache_dsa_kv, cache_dsa_kidx, page_table, w_in, fox_f_bias, mla_q_norm, mla_w_uq,
           mla_kv_norm, mla_w_uk, mla_w_uv, diff_lambda, diff_subln, w_out, w_ada, b_ada, ln_g, ln_b,
           router_w, router_b, exp_w_gu, exp_b_gu, exp_w_dn, exp_b_dn, t5_table):
    depth = w_in.shape[0]
    alpha = (2 * depth) ** 0.25
    bsz, bd = x_prompt.shape[0], x_sample.shape[0]
    caches_t = (jnp.transpose(cache_fox_kv, (0, 1, 3, 4, 2)), jnp.transpose(cache_fox_logf, (0, 1, 3, 2)),
                cache_mla_ckv, jnp.transpose(cache_mla_krope, (0, 1, 3, 2)),
                jnp.transpose(cache_diff_kv, (0, 1, 3, 4, 2)), jnp.transpose(cache_dsa_kv, (0, 1, 3, 4, 2)),
                jnp.transpose(cache_dsa_kidx, (0, 1, 3, 2)))
    n_c = bsz + bd
    c_all = jnp.pad(jnp.concatenate([c_prompt, c_sample], axis=0), ((0, -n_c % 8), (0, 0)))
    mod_all = _ada(c_all, w_ada, b_ada)
    ew = (exp_w_gu, exp_b_gu, exp_w_dn, exp_b_dn)
    xp, xs = x_prompt, x_sample
    rows_p, rows_s = [], []
    for l in range(depth):
        wts = _prep_layer_weights(l, w_in, fox_f_bias, mla_q_norm, mla_w_uq, mla_kv_norm, mla_w_uk, mla_w_uv,
                                  diff_subln, w_out, router_w, router_b)
        xp, rp = _prompt_layer(l, xp, mod_all[l, :bsz], wts, w_out[l], ln_g, ln_b, ew, t5_table, diff_lambda, alpha)
        xs, rs = _sample_layer(l, xs, mod_all[l, bsz:n_c], wts, w_out[l], ln_g, ln_b, ew, t5_table, diff_lambda,
                               diff_subln, alpha, caches_t, page_table)
        rows_p.append(rp)
        rows_s.append(rs)

    def stack(rows, name):
        return jnp.stack([r[name] for r in rows], axis=0)

    out = [xp, xs]
    for name in ('fox_kv', 'fox_logf', 'mla_ckv', 'mla_krope', 'diff_kv', 'dsa_kv', 'dsa_kidx'):
        out += [stack(rows_p, name), stack(rows_s, name)]
    return tuple(out)
```

```python
import functools
import math

import numpy as np
import jax
import jax.numpy as jnp
from jax import lax
from jax.experimental import pallas as pl
from jax.experimental.pallas import tpu as pltpu

F32 = jnp.float32
BF16 = jnp.bfloat16
I32 = jnp.int32

LANES = 128
HEAD_DIM = 64
N_HEADS = 4
MLA_NOPE, MLA_ROPE, MLA_V, MLA_KV_RANK, MLA_Q_RANK = 64, 32, 64, 128, 192
DIFF_HALF = HEAD_DIM // 2
IDX_HEADS, IDX_DIM, IDX_TOPK = 8, 32, 256
ROPE_BASE = 10000.0
N_BUCKETS, MAX_DISTANCE = 32, 128
N_EXPERTS, TOP_K, EXPERT_BLOCK = 32, 4, 128
SWIGLU_LIMIT, SWIGLU_ALPHA = 7.0, 1.702
NEG_INF = -1e30
EPS = 1e-5
INT_MIN = -(2 ** 31)

IN_SIZES = (256, 64, 64, 4, 192, 128, 32, 256, 64, 64, 256, 64, 64, 256, 32, 8)
IN_OFFS = tuple(int(v) for v in np.cumsum((0,) + IN_SIZES))

S_FQ, S_FKV, S_FF, S_CQ, S_CKV, S_KR, S_DQ, S_DKV, S_SQ, S_SKV, S_IQ, S_IK, S_IW = (
    0, 4, 5, 6, 8, 9, 10, 18, 19, 23, 24, 32, 33)
N_SLABS = 34

ROW_TILE = 256
Q_TILE = 256
FINAL_TILE = 128
PAGES_PER_STEP = 32
VMEM_LIMIT = 56 * 1024 * 1024


def _cparams(sem, vmem=VMEM_LIMIT):
    return pltpu.CompilerParams(dimension_semantics=sem, vmem_limit_bytes=vmem)


def _bdot(a, b):
    return jnp.dot(a.astype(BF16), b.astype(BF16), preferred_element_type=F32)


def _bdot_nt(a, b):
    return lax.dot_general(a.astype(BF16), b.astype(BF16), (((1,), (1,)), ((), ())),
                           preferred_element_type=F32)


def _layer_norm(z, g, b):
    mu = jnp.mean(z, axis=-1, keepdims=True)
    var = jnp.mean(jnp.square(z - mu), axis=-1, keepdims=True)
    return (z - mu) * lax.rsqrt(var + EPS) * g + b


def _in_proj_columns():
    idx = np.full((N_SLABS * LANES,), -1, np.int64)

    def put(slab, lane0, src0, n):
        idx[slab * LANES + lane0: slab * LANES + lane0 + n] = np.arange(src0, src0 + n)

    o = IN_OFFS
    for h in range(N_HEADS):
        put(S_FQ + h, 0, o[0] + h * HEAD_DIM, HEAD_DIM)
        put(S_DQ + 2 * h, 0, o[7] + h * HEAD_DIM, DIFF_HALF)
        put(S_DQ + 2 * h + 1, DIFF_HALF, o[7] + h * HEAD_DIM + DIFF_HALF, DIFF_HALF)
        put(S_SQ + h, 0, o[10] + h * HEAD_DIM, HEAD_DIM)
    put(S_FKV, 0, o[1], 2 * HEAD_DIM)
    put(S_FF, 0, o[3], N_HEADS)
    put(S_CQ, 0, o[4], MLA_Q_RANK)
    put(S_CKV, 0, o[5], MLA_KV_RANK)
    put(S_KR, 0, o[6], MLA_ROPE)
    put(S_DKV, 0, o[8], 2 * HEAD_DIM)
    put(S_SKV, 0, o[11], 2 * HEAD_DIM)
    for h in range(IDX_HEADS):
        put(S_IQ + h, 0, o[13] + h * IDX_DIM, IDX_DIM)
    put(S_IK, 0, o[14], IDX_DIM)
    put(S_IW, 0, o[15], IDX_HEADS)
    return idx


def _column_runs(idx):
    runs, i = [], 0
    while i < len(idx):
        j = i + 1
        if idx[i] < 0:
            while j < len(idx) and idx[j] < 0:
                j += 1
            runs.append((0, j - i, True))
        else:
            while j < len(idx) and idx[j] == idx[j - 1] + 1:
                j += 1
            runs.append((int(idx[i]), j - i, False))
        i = j
    return tuple(runs)


_IN_RUNS = _column_runs(_in_proj_columns())


def _pad_last(a, n):
    return jnp.pad(a, [(0, 0)] * (a.ndim - 1) + [(0, n - a.shape[-1])])


def _prep_layer_weights(l, w_in, fox_f_bias, mla_q_norm, mla_w_uq, mla_kv_norm, mla_w_uk, mla_w_uv,
                        diff_subln, w_out, router_w, router_b):
    pieces = []
    for start, n, is_zero in _IN_RUNS:
        if is_zero:
            pieces.append(jnp.zeros((w_in.shape[1], n), w_in.dtype))
        else:
            pieces.append(w_in[l, :, start:start + n])
    w_in_p = jnp.concatenate(pieces, axis=1).astype(BF16)
    fb = _pad_last(fox_f_bias[l][None, :], LANES)
    gq = _pad_last(mla_q_norm[l][None, :], 2 * LANES)
    gkv = mla_kv_norm[l][None, :]
    wuq = mla_w_uq[l].reshape(MLA_Q_RANK, N_HEADS, MLA_NOPE + MLA_ROPE)
    nope = _pad_last(wuq[:, :, :MLA_NOPE], LANES).reshape(MLA_Q_RANK, N_HEADS * LANES)
    ropew = _pad_last(wuq[:, :, MLA_NOPE:], LANES).reshape(MLA_Q_RANK, N_HEADS * LANES)
    wuq_p = jnp.pad(jnp.concatenate([nope, ropew], axis=1), ((0, 2 * LANES - MLA_Q_RANK), (0, 0))).astype(BF16)
    wk = jnp.transpose(mla_w_uk[l], (1, 2, 0))
    wk = jnp.pad(wk, ((0, 0), (0, LANES - MLA_NOPE), (0, 0))).astype(BF16)
    wv = jnp.transpose(mla_w_uv[l], (1, 0, 2))
    wv = _pad_last(wv, LANES).astype(BF16)
    subln = jnp.pad(diff_subln[l][None, :], ((0, 0), (HEAD_DIM, 0)))
    rw = _pad_last(router_w[l], LANES)
    rb = jnp.pad(router_b[l][None, :], ((0, 0), (0, LANES - N_EXPERTS)), constant_values=-3e38)
    return dict(w_in_p=w_in_p, fb=fb, gq=gq, gkv=gkv, wuq_p=wuq_p, wk=wk, wv=wv, subln=subln, rw=rw, rb=rb)


def _prep_w_out(w_out_l, value_in_upper_half):
    w = w_out_l.reshape(4, N_HEADS, HEAD_DIM, -1)
    lo = jnp.pad(w, ((0, 0), (0, 0), (0, HEAD_DIM), (0, 0)))
    hi = jnp.pad(w, ((0, 0), (0, 0), (HEAD_DIM, 0), (0, 0)))
    sel = hi if value_in_upper_half else lo
    out = jnp.stack([sel[0], lo[1], sel[2], sel[3]], axis=0)
    return out.reshape(16, LANES, -1).astype(BF16)


def _t5_bucket(rel):
    n = jnp.maximum(rel, 0)
    max_exact = N_BUCKETS // 2
    nf = jnp.maximum(n, 1).astype(F32)
    large = max_exact + (jnp.log(nf / max_exact) / math.log(MAX_DISTANCE / max_exact)
                         * (N_BUCKETS - max_exact)).astype(I32)
    return jnp.where(n < max_exact, n, jnp.minimum(large, N_BUCKETS - 1))


def _t5_rows(table_cols, rel, reps):
    hot = (_t5_bucket(rel)[..., None] == jnp.arange(N_BUCKETS, dtype=I32)).astype(F32)
    b = jnp.einsum('rck,kh->hrc', hot, table_cols.astype(F32), precision=lax.Precision.HIGHEST)
    b = b.reshape(N_HEADS * rel.shape[0], rel.shape[1])
    return jnp.concatenate([b] * reps, axis=0)


def _rope_tables(pos):
    half = MLA_ROPE // 2
    inv_freq = ROPE_BASE ** (-jnp.arange(half, dtype=F32) / half)
    ang = pos.astype(F32)[:, None] * inv_freq[None, :]
    cos, sin = jnp.cos(ang), jnp.sin(ang)
    cos_t = _pad_last(jnp.concatenate([cos, cos], axis=1), LANES)
    sin_t = _pad_last(jnp.concatenate([-sin, sin], axis=1), LANES)
    return cos_t, sin_t


def _ada_kernel(c_ref, w_ref, b_ref, o_ref):
    c = c_ref[...]
    o_ref[0] = _bdot(c * jax.nn.sigmoid(c), w_ref[0]) + b_ref[0]


def _ada(c_all, w_ada, b_ada):
    depth, d, n = w_ada.shape
    rows = c_all.shape[0]
    tn = n // 4 if n % (4 * LANES) == 0 else n
    return pl.pallas_call(
        _ada_kernel,
        grid=(depth, n // tn),
        in_specs=[pl.BlockSpec((rows, d), lambda l, j: (0, 0)),
                  pl.BlockSpec((1, d, tn), lambda l, j: (l, 0, j)),
                  pl.BlockSpec((1, 1, tn), lambda l, j: (l, 0, j))],
        out_specs=pl.BlockSpec((1, rows, tn), lambda l, j: (l, 0, j)),
        out_shape=jax.ShapeDtypeStruct((depth, rows, n), F32),
        compiler_params=_cparams(("arbitrary", "arbitrary")),
        name="ada_mod",
    )(c_all, w_ada, b_ada.reshape(depth, 1, n))


def _rope_apply(x, cos_t, sin_t):
    lane = lax.broadcasted_iota(I32, x.shape, 1)
    first_half = (lane % MLA_ROPE) < (MLA_ROPE // 2)
    swapped = jnp.where(first_half, pltpu.roll(x, LANES - MLA_ROPE // 2, 1), pltpu.roll(x, MLA_ROPE // 2, 1))
    return x * cos_t + swapped * sin_t


def _proj_kernel(*refs, seq, tr, transposed):
    (x_ref, sc_ref, sh_ref, w_ref, fb_ref, gq_ref, wuq_ref, gkv_ref, wk_ref, cos_ref, sin_ref) = refs[:11]
    out_refs, carry_ref = refs[11:-1], refs[-1]
    r = pl.program_id(0)
    h = x_ref[...] * (1.0 + sc_ref[0]) + sh_ref[0]
    p = _bdot(h, w_ref[...])

    def slab(i, n=1):
        return p[:, i * LANES:(i + n) * LANES]

    lane = lax.broadcasted_iota(I32, (tr, LANES), 1)
    ff = slab(S_FF) + fb_ref[...]
    logf = jnp.minimum(ff, 0.0) - jnp.log(1.0 + jnp.exp(-jnp.abs(ff)))
    logf = jnp.where(lane < N_HEADS, logf, 0.0)

    cq = slab(S_CQ, 2)
    cqn = cq * lax.rsqrt(jnp.sum(cq * cq, axis=-1, keepdims=True) * (1.0 / MLA_Q_RANK) + EPS) * gq_ref[...]
    q_mla = _bdot(cqn, wuq_ref[...])
    cos_t, sin_t = cos_ref[...], sin_ref[...]
    qm_slabs = []
    for hd in range(N_HEADS):
        q_nope = q_mla[:, hd * LANES:(hd + 1) * LANES]
        qm_slabs.append(_bdot(q_nope, wk_ref[hd]))
        q_rope = q_mla[:, (N_HEADS + hd) * LANES:(N_HEADS + hd + 1) * LANES]
        qm_slabs.append(_rope_apply(q_rope, cos_t, sin_t))

    ckv = slab(S_CKV)
    ckvn = ckv * lax.rsqrt(jnp.mean(ckv * ckv, axis=-1, keepdims=True) + EPS) * gkv_ref[...]
    krope = _rope_apply(slab(S_KR), cos_t, sin_t)

    rows_ref = out_refs[0]
    for i, v in enumerate((slab(S_FKV), logf, ckvn, krope, slab(S_DKV), slab(S_SKV), slab(S_IK))):
        rows_ref[:, i * LANES:(i + 1) * LANES] = v

    ti = lax.broadcasted_iota(I32, (tr, tr), 0)
    tj = lax.broadcasted_iota(I32, (tr, tr), 1)
    if tr <= seq:
        tri = tj <= ti
    else:
        tri = (tj <= ti) & ((tj // seq) == (ti // seq))
    cum = jnp.dot(tri.astype(F32), logf, precision=lax.Precision.HIGHEST, preferred_element_type=F32)
    if tr <= seq:
        @pl.when(r % (seq // tr) == 0)
        def _():
            carry_ref[...] = jnp.zeros_like(carry_ref)
        cum = cum + carry_ref[0:1, :]
        carry_ref[...] = jnp.broadcast_to(cum[tr - 1:tr, :], carry_ref.shape)
    out_refs[1][...] = cum

    if not transposed:
        qf_ref, qm_ref, qd_ref, qs_ref, qi_ref, iw_ref = out_refs[2:]
        qf_ref[...] = slab(S_FQ, N_HEADS).astype(BF16)
        for i, v in enumerate(qm_slabs):
            qm_ref[:, i * LANES:(i + 1) * LANES] = v.astype(BF16)
        qd_ref[...] = slab(S_DQ, 2 * N_HEADS).astype(BF16)
        qs_ref[...] = slab(S_SQ, N_HEADS).astype(BF16)
        qi_ref[...] = slab(S_IQ, IDX_HEADS).astype(BF16)
        iw_ref[...] = slab(S_IW)
        return

    kvb_ref, vt_ref, qt_ref, aux_ref = out_refs[2:]
    hi = cum.astype(BF16).astype(F32)
    mid = (cum - hi).astype(BF16).astype(F32)
    lo = cum - hi - mid
    ones = jnp.where((lane >= 3 * N_HEADS) & (lane < 3 * N_HEADS + 3), 1.0, 0.0)
    aug = hi + pltpu.roll(mid, N_HEADS, 1) + pltpu.roll(lo, 2 * N_HEADS, 1) + ones
    for i, v in enumerate((ckvn, krope, slab(S_FKV), aug, slab(S_DKV), aug, slab(S_SKV), slab(S_IK))):
        kvb_ref[:, i * LANES:(i + 1) * LANES] = v.astype(BF16)
    for i, v in enumerate((ckvn, slab(S_FKV), slab(S_DKV), slab(S_SKV))):
        vt_ref[0, i * LANES:(i + 1) * LANES, :] = v.T.astype(BF16)
    q_list = ([v * (MLA_NOPE + MLA_ROPE) ** -0.5 for v in qm_slabs]
              + [slab(S_DQ + i) * DIFF_HALF ** -0.5 for i in range(2 * N_HEADS)]
              + [slab(S_IQ + i) for i in range(IDX_HEADS)]
              + [slab(S_FQ + i) * HEAD_DIM ** -0.5 for i in range(N_HEADS)]
              + [slab(S_SQ + i) * HEAD_DIM ** -0.5 for i in range(N_HEADS)])
    for i, v in enumerate(q_list):
        qt_ref[0, i * LANES:(i + 1) * LANES, :] = v.T.astype(BF16)
    aux_ref[0, 0:LANES, :] = cum.T
    aux_ref[0, LANES:2 * LANES, :] = slab(S_IW).T


def _tile_mod(a, tr):
    return a if a.ndim == 3 else a.reshape(a.shape[0] // tr, tr, a.shape[1])


def _mod_map(mr, tr, rows_per_mod):
    if mr == 1:
        return lambda r, *_: (r * tr // rows_per_mod, 0, 0)
    return lambda r, *_: (r, 0, 0)


def _project(x2d, scale_b, shift_b, wts, cos_t, sin_t, seq, rows_per_mod, transposed):
    rtot, d = x2d.shape
    tr = min(ROW_TILE, rtot)
    assert rtot % tr == 0 and (seq % tr == 0 or tr % seq == 0)
    scale_b, shift_b = _tile_mod(scale_b, tr), _tile_mod(shift_b, tr)
    mr = scale_b.shape[1]
    mod_map = _mod_map(mr, tr, rows_per_mod)
    full = lambda a: pl.BlockSpec(a.shape, lambda r: (0,) * a.ndim)
    row = lambda n: pl.BlockSpec((tr, n), lambda r: (r, 0))
    outs = [(7 * LANES, F32), (LANES, F32)]
    if transposed:
        outs += [(8 * LANES, BF16)]
        touts = [(4 * LANES, BF16), (32 * LANES, BF16), (2 * LANES, F32)]
    else:
        outs += [(4 * LANES, BF16), (8 * LANES, BF16), (8 * LANES, BF16), (4 * LANES, BF16), (8 * LANES, BF16),
                 (LANES, F32)]
        touts = []
    return pl.pallas_call(
        functools.partial(_proj_kernel, seq=seq, tr=tr, transposed=transposed),
        grid=(rtot // tr,),
        in_specs=[row(d), pl.BlockSpec((1, mr, d), mod_map), pl.BlockSpec((1, mr, d), mod_map),
                  full(wts['w_in_p']), full(wts['fb']), full(wts['gq']), full(wts['wuq_p']), full(wts['gkv']),
                  full(wts['wk']), row(LANES), row(LANES)],
        out_specs=[row(n) for n, _ in outs] + [pl.BlockSpec((1, n, tr), lambda r: (r, 0, 0)) for n, _ in touts],
        out_shape=[jax.ShapeDtypeStruct((rtot, n), dt) for n, dt in outs]
        + [jax.ShapeDtypeStruct((rtot // tr, n, tr), dt) for n, dt in touts],
        scratch_shapes=[pltpu.VMEM((8, LANES), F32)],
        compiler_params=_cparams(("arbitrary",)),
        name="in_proj",
    )(x2d, scale_b, shift_b, wts['w_in_p'], wts['fb'], wts['gq'], wts['wuq_p'], wts['gkv'], wts['wk'],
      cos_t, sin_t)


def _col_online(st, vt, m_ref, l_ref, acc_ref, keep=None):
    m_prev = m_ref[...]
    m_new = jnp.maximum(m_prev, jnp.max(st, axis=0, keepdims=True))
    alpha = jnp.exp(m_prev - m_new)
    p = jnp.exp(st - m_new)
    if keep is not None:
        p = jnp.where(keep, p, 0.0)
    l_ref[...] = alpha * l_ref[...] + jnp.sum(p, axis=0, keepdims=True)
    acc_ref[...] = alpha * acc_ref[...] + jnp.dot(vt, p.astype(BF16), preferred_element_type=F32)
    m_ref[...] = m_new


def _causal_t(tk, n, tq):
    si = lax.broadcasted_iota(I32, (tk, n), 0)
    ti = lax.broadcasted_iota(I32, (tk, n), 1) % tq
    return si <= ti


def _store_heads_t(o_ref, ot, tq):
    for h in range(N_HEADS):
        o_ref[h] = ot[:, h * tq:(h + 1) * tq].T.astype(BF16)


def _init_state(m_ref, l_ref, acc_ref):
    m_ref[...] = jnp.full(m_ref.shape, NEG_INF, F32)
    l_ref[...] = jnp.zeros(l_ref.shape, F32)
    acc_ref[...] = jnp.zeros(acc_ref.shape, F32)


def _split3(x):
    hi = x.astype(BF16).astype(F32)
    mid = (x - hi).astype(BF16).astype(F32)
    return hi, mid, x - hi - mid


def _const_rows(row):
    hi, mid, lo = _split3(row)
    ri = lax.broadcasted_iota(I32, (LANES, row.shape[1]), 0)
    blk = jnp.where(ri == 3 * N_HEADS, hi, 0.0)
    blk = jnp.where(ri == 3 * N_HEADS + 1, mid, blk)
    blk = jnp.where(ri == 3 * N_HEADS + 2, lo, blk)
    return blk.astype(BF16)


def _fox_kernel(qt_ref, aux_ref, k_ref, vt_ref, o_ref, qs_ref, m_ref, l_ref, acc_ref, *, tq):
    i = pl.program_id(1)
    ri = lax.broadcasted_iota(I32, (LANES, tq), 0)
    for h in range(N_HEADS):
        qs_ref[0:LANES, h * tq:(h + 1) * tq] = qt_ref[h * LANES:(h + 1) * LANES, :]
        hi, mid, lo = _split3(aux_ref[h:h + 1, :])
        blk = jnp.where((ri < 3 * N_HEADS) & (ri % N_HEADS == h), -1.0, 0.0)
        blk = jnp.where(ri == 3 * N_HEADS, hi, blk)
        blk = jnp.where(ri == 3 * N_HEADS + 1, mid, blk)
        blk = jnp.where(ri == 3 * N_HEADS + 2, lo, blk)
        qs_ref[LANES:2 * LANES, h * tq:(h + 1) * tq] = blk.astype(BF16)
    _init_state(m_ref, l_ref, acc_ref)

    def step(j, diag):
        k = k_ref[pl.ds(pl.multiple_of(j * tq, tq), tq), :]
        st = jnp.dot(k, qs_ref[...], preferred_element_type=F32)
        if diag:
            st = jnp.where(_causal_t(tq, N_HEADS * tq, tq), st, NEG_INF)
        _col_online(st, vt_ref[j], m_ref, l_ref, acc_ref)

    def body(j, c):
        step(j, False)
        return c

    lax.fori_loop(0, i, body, 0)
    step(i, True)
    _store_heads_t(o_ref, acc_ref[...] / l_ref[...], tq)


def _mla_kernel(qt_ref, k_ref, vt_ref, o_ref, qs_ref, m_ref, l_ref, acc_ref, *, tq):
    i = pl.program_id(1)
    for h in range(N_HEADS):
        qs_ref[:, h * tq:(h + 1) * tq] = qt_ref[h * 2 * LANES:(h + 1) * 2 * LANES, :]
    _init_state(m_ref, l_ref, acc_ref)

    def step(j, diag):
        k = k_ref[pl.ds(pl.multiple_of(j * tq, tq), tq), :]
        st = jnp.dot(k, qs_ref[...], preferred_element_type=F32)
        if diag:
            st = jnp.where(_causal_t(tq, N_HEADS * tq, tq), st, NEG_INF)
        _col_online(st, vt_ref[j], m_ref, l_ref, acc_ref)

    def body(j, c):
        step(j, False)
        return c

    lax.fori_loop(0, i, body, 0)
    step(i, True)
    _store_heads_t(o_ref, acc_ref[...] / l_ref[...], tq)


def _near_far_loop(i, step):
    def body(j, c):
        step(j, 2)
        return c

    lax.fori_loop(0, jnp.maximum(i - 1, 0), body, 0)

    @pl.when(i >= 1)
    def _():
        step(i - 1, 1)

    step(i, 0)


def _diff_kernel(lam_ref, qt_ref, k_ref, vt_ref, tb_ref, tfar_ref, g_ref, o_ref, qs_ref, m_ref, l_ref, acc_ref,
                 *, tq, lam_init):
    i = pl.program_id(1)
    nq = N_HEADS * tq
    for g in range(2):
        for h in range(N_HEADS):
            qs_ref[0:LANES, (g * N_HEADS + h) * tq:(g * N_HEADS + h + 1) * tq] = \
                qt_ref[(2 * h + g) * LANES:(2 * h + g + 1) * LANES, :]
    qs_ref[LANES:2 * LANES, :] = _const_rows(tfar_ref[...])
    _init_state(m_ref, l_ref, acc_ref)

    def step(j, dist):
        k = k_ref[pl.ds(pl.multiple_of(j * tq, tq), tq), :]
        st = jnp.dot(k, qs_ref[...], preferred_element_type=F32)
        if dist < 2:
            st = st + tb_ref[dist]
        if dist == 0:
            st = jnp.where(_causal_t(tq, 2 * nq, tq), st, NEG_INF)
        _col_online(st, vt_ref[j], m_ref, l_ref, acc_ref)

    _near_far_loop(i, step)
    ot = acc_ref[...] / l_ref[...]
    ot = ot[:, :nq] - lam_ref[0] * ot[:, nq:]
    ri = lax.broadcasted_iota(I32, ot.shape, 0)
    ot = jnp.where(ri >= HEAD_DIM, ot, 0.0)
    y = ot * lax.rsqrt(jnp.sum(ot * ot, axis=0, keepdims=True) * (1.0 / HEAD_DIM) + EPS) * g_ref[...]
    _store_heads_t(o_ref, y * (1.0 - lam_init), tq)


def _score_keys(sc):
    bits = lax.bitcast_convert_type(sc, I32)
    key = jnp.where(bits < 0, bits ^ jnp.int32(0x7FFFFFFF), bits)
    return jnp.where(sc == 0.0, 0, key)


def _dsa_kernel(qit_ref, iwt_ref, ki_ref, qt_ref, k_ref, vt_ref, tb_ref, tfar_ref, o_ref,
                keys_ref, qis_ref, wr_ref, thr_ref, qs_ref, m_ref, l_ref, acc_ref, *, tq, nsel):
    i = pl.program_id(1)
    for h in range(IDX_HEADS):
        qis_ref[:, h * tq:(h + 1) * tq] = qit_ref[h * LANES:(h + 1) * LANES, :]
        wr_ref[:, h * tq:(h + 1) * tq] = iwt_ref[h:h + 1, :] * IDX_HEADS ** -0.5
    for h in range(N_HEADS):
        qs_ref[:, h * tq:(h + 1) * tq] = qt_ref[h * LANES:(h + 1) * LANES, :]
    _init_state(m_ref, l_ref, acc_ref)

    def score_tile(j, diag):
        kt = ki_ref[pl.ds(pl.multiple_of(j * tq, tq), tq), :]
        sc = None
        for h in range(IDX_HEADS):
            cs = slice(h * tq, (h + 1) * tq)
            r = jnp.dot(kt, qis_ref[:, cs], preferred_element_type=F32) * IDX_DIM ** -0.5
            r = jnp.maximum(r, 0.0) * wr_ref[:, cs]
            sc = r if sc is None else sc + r
        key = _score_keys(sc)
        if diag:
            key = jnp.where(_causal_t(tq, tq, tq), key, INT_MIN)
        keys_ref[j] = key

    def sbody(j, c):
        score_tile(j, False)
        return c

    lax.fori_loop(0, i, sbody, 0)
    score_tile(i, True)

    def count_ge(cand):
        def cbody(j, c):
            ge = jnp.where(keys_ref[j] >= cand, 1.0, 0.0)
            return c + jnp.sum(ge.reshape(tq // 8, 8, tq), axis=0)

        c = lax.fori_loop(0, i + 1, cbody, jnp.zeros((8, tq), F32))
        return jnp.sum(c, axis=0, keepdims=True)

    zero = jnp.zeros((1, tq), I32)
    t0 = jnp.where(count_ge(zero) >= nsel, zero, INT_MIN)

    def bit_body(b, t):
        cand = t + jnp.left_shift(jnp.int32(1), 30 - b)
        return jnp.where(count_ge(cand) >= nsel, cand, t)

    thr_ref[...] = lax.fori_loop(0, 31, bit_body, t0)

    def step(j, dist):
        k = k_ref[pl.ds(pl.multiple_of(j * tq, tq), tq), :]
        st = jnp.dot(k, qs_ref[...], preferred_element_type=F32) + tfar_ref[...]
        if dist < 2:
            st = st + tb_ref[dist]
        sel = keys_ref[j] >= thr_ref[...]
        if dist == 0:
            sel = sel & _causal_t(tq, tq, tq)
        self32 = jnp.where(sel, 1.0, 0.0)
        keep = jnp.concatenate([self32] * N_HEADS, axis=1) > 0.5
        st = jnp.where(keep, st, NEG_INF)
        _col_online(st, vt_ref[j], m_ref, l_ref, acc_ref, keep=keep)

    _near_far_loop(i, step)
    _store_heads_t(o_ref, acc_ref[...] / l_ref[...], tq)


def _prompt_attention(pr, t5_table, lam, lam_init, subln, bsz, seq):
    _, _, kvb, vt, qt, aux = pr
    tq = min(Q_TILE, seq)
    assert seq % tq == 0 and tq == min(ROW_TILE, bsz * seq)
    nt = seq // tq
    kvb3 = kvb.reshape(bsz, seq, kvb.shape[-1])
    vt4 = vt.reshape(bsz, nt, vt.shape[1], tq)

    qspec = lambda n, blk: pl.BlockSpec((None, n * LANES, tq), lambda b, i: (b * nt + i, blk, 0))
    auxspec = lambda blk: pl.BlockSpec((None, LANES, tq), lambda b, i: (b * nt + i, blk, 0))
    kspec = lambda n, blk: pl.BlockSpec((None, seq, n * LANES), lambda b, i: (b, 0, blk))
    vspec = lambda blk: pl.BlockSpec((None, nt, LANES, tq), lambda b, i: (b, 0, blk, 0))
    ospec = pl.BlockSpec((None, N_HEADS, tq, LANES), lambda b, i: (b, 0, i, 0))
    oshape = jax.ShapeDtypeStruct((bsz, N_HEADS, seq, LANES), BF16)
    rowv = lambda n: pltpu.VMEM((1, n), F32)
    grid = (bsz, nt)
    sem = ("arbitrary", "arbitrary")
    nq = N_HEADS * tq

    o_fox = pl.pallas_call(
        functools.partial(_fox_kernel, tq=tq), grid=grid,
        in_specs=[qspec(4, 6), auxspec(0), kspec(2, 1), vspec(1)],
        out_specs=ospec, out_shape=oshape,
        scratch_shapes=[pltpu.VMEM((2 * LANES, nq), BF16), rowv(nq), rowv(nq), pltpu.VMEM((LANES, nq), F32)],
        compiler_params=_cparams(sem), name="fox_attention",
    )(qt, aux, kvb3, vt4)

    o_mla = pl.pallas_call(
        functools.partial(_mla_kernel, tq=tq), grid=grid,
        in_specs=[qspec(8, 0), kspec(2, 0), vspec(0)],
        out_specs=ospec, out_shape=oshape,
        scratch_shapes=[pltpu.VMEM((2 * LANES, nq), BF16), rowv(nq), rowv(nq), pltpu.VMEM((LANES, nq), F32)],
        compiler_params=_cparams(sem), name="mla_attention",
    )(qt, kvb3, vt4)

    ri = jnp.arange(tq, dtype=I32)
    rel0 = ri[:, None] - ri[None, :]
    far_rel = jnp.full((1, 1), 2 * MAX_DISTANCE, I32)

    def t5_tiles(cols, reps):
        tb = jnp.stack([_t5_rows(cols, rel0, reps).T, _t5_rows(cols, rel0 + tq, reps).T], axis=0)
        tfar = jnp.repeat(_t5_rows(cols, far_rel, reps), tq, axis=0).T
        return (tb - tfar[None]).astype(F32), tfar.astype(F32)

    assert tq >= MAX_DISTANCE
    tb_d, tfar_d = t5_tiles(t5_table[:, :N_HEADS], 2)
    tb_s, tfar_s = t5_tiles(t5_table[:, N_HEADS:], 1)
    full = lambda a: pl.BlockSpec(a.shape, lambda b, i: (0,) * a.ndim)
    gcol = subln.T

    o_diff = pl.pallas_call(
        functools.partial(_diff_kernel, tq=tq, lam_init=lam_init), grid=grid,
        in_specs=[pl.BlockSpec(memory_space=pltpu.SMEM), qspec(8, 1), kspec(2, 2), vspec(2),
                  full(tb_d), full(tfar_d), full(gcol)],
        out_specs=ospec, out_shape=oshape,
        scratch_shapes=[pltpu.VMEM((2 * LANES, 2 * nq), BF16), rowv(2 * nq), rowv(2 * nq),
                        pltpu.VMEM((LANES, 2 * nq), F32)],
        compiler_params=_cparams(sem), name="diff_attention",
    )(lam.reshape(1), qt, kvb3, vt4, tb_d, tfar_d, gcol)

    nsel = min(IDX_TOPK, seq // 4)
    o_dsa = pl.pallas_call(
        functools.partial(_dsa_kernel, tq=tq, nsel=nsel), grid=grid,
        in_specs=[qspec(8, 2), auxspec(1), kspec(1, 7), qspec(4, 7), kspec(1, 6), vspec(3),
                  full(tb_s), full(tfar_s)],
        out_specs=ospec, out_shape=oshape,
        scratch_shapes=[pltpu.VMEM((nt, tq, tq), I32), pltpu.VMEM((LANES, IDX_HEADS * tq), BF16),
                        rowv(IDX_HEADS * tq), pltpu.VMEM((1, tq), I32),
                        pltpu.VMEM((LANES, nq), BF16), rowv(nq), rowv(nq), pltpu.VMEM((LANES, nq), F32)],
        compiler_params=_cparams(sem), name="dsa_attention",
    )(qt, aux, kvb3, qt, kvb3, vt4, tb_s, tfar_s)
    return o_fox, o_mla, o_diff, o_dsa


def _out_kernel(x_ref, of_ref, om_ref, od_ref, os_ref, wo_ref, wv_ref, g1_ref, sc2_ref, sh2_ref,
                lng_ref, lnb_ref, rw_ref, rb_ref, x1_ref, h2_ref, ti_ref, tg_ref, rk_ref, cnt_ref, cnt_sc,
                *, tr, alpha):
    d = x_ref.shape[-1]
    acc = jnp.zeros((tr, d), F32)
    for h in range(N_HEADS):
        acc = acc + _bdot(of_ref[:, h].reshape(tr, LANES), wo_ref[h])
        o_mla = _bdot(om_ref[:, h].reshape(tr, LANES), wv_ref[h])
        acc = acc + _bdot(o_mla, wo_ref[N_HEADS + h])
        acc = acc + _bdot(od_ref[:, h].reshape(tr, LANES), wo_ref[2 * N_HEADS + h])
        acc = acc + _bdot(os_ref[:, h].reshape(tr, LANES), wo_ref[3 * N_HEADS + h])
    x1 = _layer_norm(alpha * x_ref[...] + g1_ref[0] * acc, lng_ref[...], lnb_ref[...])
    x1_ref[...] = x1
    h2 = x1 * (1.0 + sc2_ref[0]) + sh2_ref[0]
    h2_ref[...] = h2
    logits = jnp.dot(h2, rw_ref[...], precision=lax.Precision.HIGHEST, preferred_element_type=F32) + rb_ref[...]
    lane = lax.broadcasted_iota(I32, (tr, LANES), 1).astype(F32)
    idx_out = jnp.zeros((tr, LANES), F32)
    vals, hots = [], []
    for k in range(TOP_K):
        mx = jnp.max(logits, axis=1, keepdims=True)
        ix = jnp.min(jnp.where(logits == mx, lane, float(LANES)), axis=1, keepdims=True)
        idx_out = jnp.where(lane == k, ix, idx_out)
        vals.append(mx)
        hots.append(lane == ix)
        logits = jnp.where(hots[-1], -3.4e38, logits)
    @pl.when(pl.program_id(0) == 0)
    def _():
        cnt_sc[...] = jnp.zeros_like(cnt_sc)
    picked = sum(jnp.where(hh, 1.0, 0.0) for hh in hots)
    ti_ = lax.broadcasted_iota(I32, (tr, tr), 0)
    tj_ = lax.broadcasted_iota(I32, (tr, tr), 1)
    before = _bdot(jnp.where(tj_ < ti_, 1.0, 0.0), picked) + cnt_sc[0:1, :]
    rank_out = jnp.zeros((tr, LANES), F32)
    for k in range(TOP_K):
        rk = jnp.sum(jnp.where(hots[k], before, 0.0), axis=1, keepdims=True)
        rank_out = jnp.where(lane == k, rk, rank_out)
    rk_ref[...] = rank_out.astype(I32)
    cnt_new = cnt_sc[...] + jnp.sum(picked, axis=0, keepdims=True)
    cnt_sc[...] = cnt_new
    cnt_ref[...] = cnt_new
    es = [jnp.exp(v - vals[0]) for v in vals]
    den = es[0] + es[1] + es[2] + es[3]
    gates = jnp.zeros((tr, LANES), F32)
    for k in range(TOP_K):
        gates = jnp.where(lane == k, es[k] / den, gates)
    ti_ref[...] = idx_out.astype(I32)
    tg_ref[...] = gates


def _out_router(x2d, o_list, wo, wv, gate1, scale2, shift2, ln_g, ln_b, rw, rb, rows_per_mod, alpha):
    rtot, d = x2d.shape
    tr = min(ROW_TILE, rtot)
    nb_o, _, tt, _ = o_list[0].shape
    gate1, scale2, shift2 = _tile_mod(gate1, tr), _tile_mod(scale2, tr), _tile_mod(shift2, tr)
    mr = gate1.shape[1]
    mod_map = _mod_map(mr, tr, rows_per_mod)
    if tt >= tr:
        assert tt % tr == 0
        per = tt // tr
        ospec = pl.BlockSpec((1, N_HEADS, tr, LANES), lambda r: (r // per, 0, r % per, 0))
    else:
        assert tr % tt == 0
        ospec = pl.BlockSpec((tr // tt, N_HEADS, tt, LANES), lambda r: (r, 0, 0, 0))
    full = lambda a: pl.BlockSpec(a.shape, lambda r: (0,) * a.ndim)
    row = lambda n: pl.BlockSpec((tr, n), lambda r: (r, 0))
    mspec = pl.BlockSpec((1, mr, d), mod_map)
    return pl.pallas_call(
        functools.partial(_out_kernel, tr=tr, alpha=alpha),
        grid=(rtot // tr,),
        in_specs=[row(d), ospec, ospec, ospec, ospec, full(wo), full(wv), mspec, mspec, mspec,
                  full(ln_g), full(ln_b), full(rw), full(rb)],
        out_specs=[row(d), row(d), row(LANES), row(LANES), row(LANES), pl.BlockSpec((8, LANES), lambda r: (0, 0))],
        out_shape=[jax.ShapeDtypeStruct((rtot, d), F32), jax.ShapeDtypeStruct((rtot, d), F32),
                   jax.ShapeDtypeStruct((rtot, LANES), I32), jax.ShapeDtypeStruct((rtot, LANES), F32),
                   jax.ShapeDtypeStruct((rtot, LANES), I32), jax.ShapeDtypeStruct((8, LANES), F32)],
        scratch_shapes=[pltpu.VMEM((8, LANES), F32)],
        compiler_params=_cparams(("arbitrary",)),
        name="out_proj_router",
    )(x2d, *o_list, wo, wv, gate1, scale2, shift2, ln_g, ln_b, rw, rb)


def _routing_tables(top_idx, rank, counts, n_tok):
    a = n_tok * TOP_K
    e_flat = top_idx.reshape(a)
    padded = (counts + EXPERT_BLOCK - 1) // EXPERT_BLOCK * EXPERT_BLOCK
    pstart = jnp.cumsum(padded) - padded
    dest = (pstart[e_flat] + rank.reshape(a)).astype(I32)
    n_blocks = -(-a // EXPERT_BLOCK) + N_EXPERTS
    src = jnp.zeros((n_blocks * EXPERT_BLOCK,), I32).at[dest].set(jnp.arange(a, dtype=I32) // TOP_K)
    ends = pstart + padded
    first_row = jnp.arange(n_blocks, dtype=I32) * EXPERT_BLOCK
    blk_expert = jnp.minimum(jnp.sum((ends[None, :] <= first_row[:, None]).astype(I32), axis=1), N_EXPERTS - 1)
    return dest, src, blk_expert, n_blocks


def _moe_kernel(be_ref, src_ref, h_hbm, wgu_ref, bgu_ref, wdn_ref, bdn_ref, y_ref,
                xbuf0, xbuf1, sem, wgu_b, wdn_b, *, nblk, dff):
    i = pl.program_id(0)
    slot = i % 2

    bufs = (xbuf0, xbuf1)

    def row_copy(blk, sl, r):
        tok = src_ref[blk * EXPERT_BLOCK + r]
        return pltpu.make_async_copy(h_hbm.at[pl.ds(tok, 1)], bufs[sl].at[pl.ds(r, 1)], sem.at[sl])

    def wait_rows(blk, sl):
        def wbody(r, c):
            row_copy(blk, sl, r).wait()
            return c
        lax.fori_loop(0, EXPERT_BLOCK, wbody, 0)

    @pl.when(i == 0)
    def _():
        def body(r, c):
            row_copy(0, 0, r).start()
            return c
        lax.fori_loop(0, EXPERT_BLOCK, body, 0)

    prev = be_ref[jnp.maximum(i - 1, 0)]

    @pl.when((i == 0) | (be_ref[i] != prev))
    def _():
        wgu_b[...] = wgu_ref[0, 0].astype(BF16)
        wdn_b[...] = wdn_ref[0, 0].astype(BF16)

    nxt = jnp.minimum(i + 1, nblk - 1)
    for par in range(2):
        @pl.when(slot == par)
        def _(par=par):
            wait_rows(i, par)
            for r in range(EXPERT_BLOCK):
                row_copy(nxt, 1 - par, r).start()
            gu = jnp.dot(bufs[par][...].astype(BF16), wgu_b[...], preferred_element_type=F32) + bgu_ref[0, 0]
            g = jnp.minimum(gu[:, :dff], SWIGLU_LIMIT)
            u = jnp.clip(gu[:, dff:], -SWIGLU_LIMIT, SWIGLU_LIMIT)
            act = g * jax.nn.sigmoid(SWIGLU_ALPHA * g) * (u + 1.0)
            y_ref[...] = jnp.dot(act.astype(BF16), wdn_b[...], preferred_element_type=F32) + bdn_ref[0, 0]

            @pl.when(i == nblk - 1)
            def _():
                wait_rows(nxt, 1 - par)


def _moe_blocks(l, h2, blk_expert, src, n_blocks, exp_w_gu, exp_b_gu, exp_w_dn, exp_b_dn):
    rtot, d = h2.shape
    dff = exp_w_dn.shape[2]
    bgu = exp_b_gu.reshape(exp_b_gu.shape[0], N_EXPERTS, 1, 2 * dff)
    bdn = exp_b_dn.reshape(exp_b_dn.shape[0], N_EXPERTS, 1, d)
    gs = pltpu.PrefetchScalarGridSpec(
        num_scalar_prefetch=2,
        grid=(n_blocks,),
        in_specs=[pl.BlockSpec(memory_space=pl.ANY),
                  pl.BlockSpec((1, 1, d, 2 * dff), lambda i, be, s: (l, be[i], 0, 0)),
                  pl.BlockSpec((1, 1, 1, 2 * dff), lambda i, be, s: (l, be[i], 0, 0)),
                  pl.BlockSpec((1, 1, dff, d), lambda i, be, s: (l, be[i], 0, 0)),
                  pl.BlockSpec((1, 1, 1, d), lambda i, be, s: (l, be[i], 0, 0))],
        out_specs=pl.BlockSpec((EXPERT_BLOCK, d), lambda i, be, s: (i, 0)),
        scratch_shapes=[pltpu.VMEM((EXPERT_BLOCK, d), F32), pltpu.VMEM((EXPERT_BLOCK, d), F32),
                        pltpu.SemaphoreType.DMA((2,)),
                        pltpu.VMEM((d, 2 * dff), BF16), pltpu.VMEM((dff, d), BF16)])
    return pl.pallas_call(
        functools.partial(_moe_kernel, nblk=n_blocks, dff=dff),
        grid_spec=gs,
        out_shape=jax.ShapeDtypeStruct((n_blocks * EXPERT_BLOCK, d), F32),
        compiler_params=_cparams(("arbitrary",)),
        name="moe_experts",
    )(blk_expert, src, h2, exp_w_gu, bgu, exp_w_dn, bdn)


def _final_kernel(dest_ref, x1_ref, y_hbm, tg_ref, g2_ref, lng_ref, lnb_ref, o_ref, ybuf, sem,
                  *, nsteps, tr, alpha):
    i = pl.program_id(0)
    slot = i % 2

    def row_copy(step, sl, r, k):
        pos = dest_ref[(step * tr + r) * TOP_K + k]
        return pltpu.make_async_copy(y_hbm.at[pl.ds(pos, 1)], ybuf.at[sl, k, pl.ds(r, 1)], sem.at[sl])

    def issue(step, sl):
        def body(r, c):
            for k in range(TOP_K):
                row_copy(step, sl, r, k).start()
            return c
        lax.fori_loop(0, tr, body, 0)

    @pl.when(i == 0)
    def _():
        issue(0, 0)

    @pl.when(i + 1 < nsteps)
    def _():
        issue(i + 1, 1 - slot)

    def wbody(r, c):
        for k in range(TOP_K):
            row_copy(i, slot, r, k).wait()
        return c
    lax.fori_loop(0, tr, wbody, 0)

    tg = tg_ref[...]
    y = tg[:, 0:1] * ybuf[slot, 0]
    for k in range(1, TOP_K):
        y = y + tg[:, k:k + 1] * ybuf[slot, k]
    o_ref[...] = _layer_norm(alpha * x1_ref[...] + g2_ref[0] * y, lng_ref[...], lnb_ref[...])


def _combine_final(x1, y_pad, dest, gates, gate2, ln_g, ln_b, rows_per_mod, alpha):
    rtot, d = x1.shape
    tr = min(FINAL_TILE, rtot)
    nsteps = rtot // tr
    gate2 = _tile_mod(gate2, tr)
    mod_map = _mod_map(gate2.shape[1], tr, rows_per_mod)
    gs = pltpu.PrefetchScalarGridSpec(
        num_scalar_prefetch=1,
        grid=(nsteps,),
        in_specs=[pl.BlockSpec((tr, d), lambda r, dst: (r, 0)),
                  pl.BlockSpec(memory_space=pl.ANY),
                  pl.BlockSpec((tr, LANES), lambda r, dst: (r, 0)),
                  pl.BlockSpec((1, gate2.shape[1], d), mod_map),
                  pl.BlockSpec(ln_g.shape, lambda r, dst: (0, 0)),
                  pl.BlockSpec(ln_b.shape, lambda r, dst: (0, 0))],
        out_specs=pl.BlockSpec((tr, d), lambda r, dst: (r, 0)),
        scratch_shapes=[pltpu.VMEM((2, TOP_K, tr, d), F32), pltpu.SemaphoreType.DMA((2,))])
    return pl.pallas_call(
        functools.partial(_final_kernel, nsteps=nsteps, tr=tr, alpha=alpha),
        grid_spec=gs,
        out_shape=jax.ShapeDtypeStruct((rtot, d), F32),
        compiler_params=_cparams(("arbitrary",)),
        name="moe_combine_norm",
    )(dest, x1, y_pad, gates, gate2, ln_g, ln_b)


def _chunk_copies(specs, sem, layer, pt_ref, b, chunk, slot, pages):
    cps = []
    for ci, (cache, buf, placement) in enumerate(specs):
        for p in range(pages):
            page = pt_ref[b, chunk * pages + p]
            src = cache.at[layer, page]
            if placement == 'rows':
                dst = buf.at[slot, pl.ds(p * LANES, LANES)]
            elif placement == 'page':
                dst = buf.at[slot, :, p, :]
            else:
                idx = (slot,) + (slice(None),) * (len(buf.shape) - 2) + (pl.ds(p * LANES, LANES),)
                dst = buf.at[idx]
            cps.append(pltpu.make_async_copy(src, dst, sem.at[slot, ci]))
    return cps


def _pipeline_fetch(specs, sem, layer, pt_ref, nch, pages, chunk_of):
    b, c = pl.program_id(0), pl.program_id(1)
    nb = pl.num_programs(0)
    g = b * nch + c
    slot = g % 2

    @pl.when(g == 0)
    def _():
        for cp in _chunk_copies(specs, sem, layer, pt_ref, b, chunk_of(c), slot, pages):
            cp.start()

    @pl.when(g + 1 < nb * nch)
    def _():
        g1 = g + 1
        b1, c1 = g1 // nch, g1 % nch
        for cp in _chunk_copies(specs, sem, layer, pt_ref, b1, chunk_of(c1), 1 - slot, pages):
            cp.start()

    for cp in _chunk_copies(specs, sem, layer, pt_ref, b, chunk_of(c), slot, pages):
        cp.wait()
    return slot


def _stack_slabs(x, n, width=1):
    return jnp.concatenate([x[:, h * width * LANES:(h + 1) * width * LANES] for h in range(n)], axis=0)


def _dec_idx_kernel(pt_ref, kidx_hbm, logf_hbm, qi_ref, iw_ref, knew_ref, kp_ref, kn_ref, thr_ref, fb_ref,
                    kbuf, lbuf, sem, allk, newk, *, layer, nch, pages, dec, nsel):
    c = pl.program_id(1)
    slot = _pipeline_fetch([(kidx_hbm, kbuf, 'lanes'), (logf_hbm, lbuf, 'page')], sem, layer, pt_ref, nch, pages,
                           lambda cc: cc)
    pj = lax.broadcasted_iota(I32, (LANES, LANES), 0)
    ps = lax.broadcasted_iota(I32, (LANES, LANES), 1)
    within = jnp.where(pj >= ps, 1.0, 0.0)
    gi = lax.broadcasted_iota(I32, (pages, pages), 0)
    gj = lax.broadcasted_iota(I32, (pages, pages), 1)
    later = jnp.where(gj > gi, 1.0, 0.0)
    hp = lax.Precision.HIGHEST
    for h in range(N_HEADS):
        x = lbuf[slot, h]
        incl = jnp.dot(x, within, precision=hp, preferred_element_type=F32)
        tot = jnp.broadcast_to(incl[:, 0:1], (pages, LANES))
        fb_ref[h] = incl - x + jnp.dot(later, tot, precision=hp, preferred_element_type=F32)

    q = _stack_slabs(qi_ref[...], IDX_HEADS)[:, :IDX_DIM]
    iw = iw_ref[...]
    wcol = jnp.concatenate([iw[:, h:h + 1] for h in range(IDX_HEADS)], axis=0) * IDX_HEADS ** -0.5

    def scores(kt):
        r = jnp.maximum(_bdot(q, kt) * IDX_DIM ** -0.5, 0.0) * wcol
        sc = r[0:dec]
        for h in range(1, IDX_HEADS):
            sc = sc + r[h * dec:(h + 1) * dec]
        return _score_keys(sc)

    key = scores(kbuf[slot])
    kp_ref[...] = key
    allk[c] = key

    @pl.when(c == nch - 1)
    def _():
        kn = scores(knew_ref[...])
        t = lax.broadcasted_iota(I32, kn.shape, 0)
        s = lax.broadcasted_iota(I32, kn.shape, 1)
        kn = jnp.where(s <= t, kn, INT_MIN)
        kn_ref[...] = kn
        newk[...] = kn

        def count_ge(cand):
            def cbody(j, acc):
                ge = jnp.where(allk[j] >= cand, 1.0, 0.0)
                parts = [ge[:, i * LANES:(i + 1) * LANES] for i in range(ge.shape[1] // LANES)]
                while len(parts) > 1:
                    parts = [a + b for a, b in zip(parts[0::2], parts[1::2])] + (parts[-1:] if len(parts) % 2 else [])
                return acc + parts[0]
            acc = lax.fori_loop(0, nch, cbody, jnp.where(newk[...] >= cand, 1.0, 0.0))
            return jnp.sum(acc, axis=1, keepdims=True)

        zero = jnp.zeros((dec, 1), I32)
        t0 = jnp.where(count_ge(zero) >= nsel, zero, INT_MIN)

        def bit_body(bi, tcur):
            cand = tcur + jnp.left_shift(jnp.int32(1), 30 - bi)
            return jnp.where(count_ge(cand) >= nsel, cand, tcur)

        thr = lax.fori_loop(0, 31, bit_body, t0)
        thr_ref[...] = jnp.broadcast_to(thr, thr_ref.shape)


def _suffix_sum(x):
    n = x.shape[1]
    lane = lax.broadcasted_iota(I32, x.shape, 1)
    d = 1
    while d < n:
        x = x + jnp.where(lane + d < n, pltpu.roll(x, n - d, 1), 0.0)
        d *= 2
    return x


def _online_update_fn(s, pv, m_ref, l_ref, acc_ref, keep=None):
    m_prev = m_ref[...]
    m_new = jnp.maximum(m_prev, jnp.max(s, axis=1, keepdims=True))
    alpha = jnp.exp(m_prev - m_new)
    p = jnp.exp(s - m_new)
    if keep is not None:
        p = jnp.where(keep, p, 0.0)
    l_ref[...] = alpha * l_ref[...] + jnp.sum(p, axis=1, keepdims=True)
    acc_ref[...] = alpha * acc_ref[...] + pv(p)
    m_ref[...] = m_new


def _dec_att_kernel(pt_ref, lam_ref,
                    fox_hbm, ckv_hbm, kr_hbm, diff_hbm, dsa_hbm,
                    qf_ref, qm_ref, qd_ref, qs_ref, ecol_ref, fb_ref, kp_ref, kn_ref, thr_ref,
                    nf_ref, nl_ref, nc_ref, nk_ref, nd_ref, ns_ref,
                    tbd_ref, tbs_ref, tnd_ref, tns_ref, g_ref,
                    of_ref, om_ref, od_ref, os_ref,
                    fbuf, cbuf, kbuf, dbuf, sbuf, sem,
                    mf, lf_, af, mm, lm, am, md, ld, ad, ms, ls, as_, rcar,
                    *, layer, nch, pages, dec, lam_init):
    c = pl.program_id(1)
    specs = [(fox_hbm, fbuf, 'lanes'), (ckv_hbm, cbuf, 'rows'), (kr_hbm, kbuf, 'lanes'),
             (diff_hbm, dbuf, 'lanes'), (dsa_hbm, sbuf, 'lanes')]
    slot = _pipeline_fetch(specs, sem, layer, pt_ref, nch, pages, lambda cc: nch - 1 - cc)
    nr = N_HEADS * dec

    qf = _stack_slabs(qf_ref[...], N_HEADS)[:, :HEAD_DIM]
    qm = _stack_slabs(qm_ref[...], N_HEADS, 2)
    qlat, qrope = qm[:, :LANES], qm[:, LANES:LANES + MLA_ROPE]
    qd_all = qd_ref[...]
    qd = jnp.concatenate([qd_all[:, (2 * h + g) * LANES:(2 * h + g + 1) * LANES]
                          for g in range(2) for h in range(N_HEADS)], axis=0)[:, :HEAD_DIM]
    qs = _stack_slabs(qs_ref[...], N_HEADS)[:, :HEAD_DIM]
    ecol = ecol_ref[...]
    thr = thr_ref[:, 0:1]

    def process(fkv, after, ckv, krt, dkv, skv, keys, t5d, t5s, is_new):
        nk = after.shape[1]
        if is_new:
            trow = lax.broadcasted_iota(I32, (dec, nk), 0)
            scol = lax.broadcasted_iota(I32, (dec, nk), 1)
            ok8 = jnp.where(scol <= trow, 1.0, 0.0)
            ok = jnp.concatenate([ok8] * N_HEADS, axis=0) > 0.5
            ok2 = jnp.concatenate([ok8] * (2 * N_HEADS), axis=0) > 0.5
        s = _bdot(qf, fkv[0]) * HEAD_DIM ** -0.5
        excl = after + rcar[...]
        bias = jnp.concatenate([jnp.broadcast_to(excl[h:h + 1, :], (dec, nk)) for h in range(N_HEADS)], axis=0)
        s = s + bias - ecol
        if is_new:
            s = jnp.where(ok, s, NEG_INF)
        _online_update_fn(s, lambda p: _bdot_nt(p, fkv[1]), mf, lf_, af)
        s = (_bdot_nt(qlat, ckv) + _bdot(qrope, krt)) * (MLA_NOPE + MLA_ROPE) ** -0.5
        if is_new:
            s = jnp.where(ok, s, NEG_INF)
        _online_update_fn(s, lambda p: _bdot(p, ckv), mm, lm, am)
        s = _bdot(qd, dkv[0]) * DIFF_HALF ** -0.5 + t5d
        if is_new:
            s = jnp.where(ok2, s, NEG_INF)
        _online_update_fn(s, lambda p: _bdot_nt(p, dkv[1]), md, ld, ad)
        s = _bdot(qs, skv[0]) * HEAD_DIM ** -0.5 + t5s
        sel8 = jnp.where(keys >= thr, 1.0, 0.0)
        if is_new:
            sel8 = sel8 * ok8
        keep = jnp.concatenate([sel8] * N_HEADS, axis=0) > 0.5
        s = jnp.where(keep, s, NEG_INF)
        _online_update_fn(s, lambda p: _bdot_nt(p, skv[1]), ms, ls, as_, keep=keep)

    @pl.when(c == 0)
    def _():
        for m_, l_, a_ in ((mf, lf_, af), (mm, lm, am), (md, ld, ad), (ms, ls, as_)):
            _init_state(m_, l_, a_)
        rcar[...] = jnp.zeros_like(rcar)
        lgf_new = nl_ref[...]
        incl_new = _suffix_sum(lgf_new)
        process(nf_ref[...], incl_new - lgf_new, nc_ref[...], nk_ref[...], nd_ref[...], ns_ref[...], kn_ref[...],
                tnd_ref[...], tns_ref[...], True)
        rcar[...] = incl_new[:, 0:1]

    process(fbuf[slot], fb_ref[...], cbuf[slot], kbuf[slot], dbuf[slot], sbuf[slot], kp_ref[...],
            tbd_ref[...], tbs_ref[...], False)

    @pl.when(c == nch - 1)
    def _():
        def put(o_ref, o):
            w = o.shape[1]
            for h in range(N_HEADS):
                if w < LANES:
                    o_ref[h] = jnp.zeros((dec, LANES), BF16)
                    o_ref[h, :, 0:w] = o[h * dec:(h + 1) * dec].astype(BF16)
                else:
                    o_ref[h] = o[h * dec:(h + 1) * dec].astype(BF16)

        put(of_ref, af[...] / lf_[...])
        put(om_ref, am[...] / lm[...])
        od = ad[...] / ld[...]
        od = od[:nr] - lam_ref[0] * od[nr:]
        y = od * lax.rsqrt(jnp.mean(od * od, axis=-1, keepdims=True) + EPS) * g_ref[...] * (1.0 - lam_init)
        put(od_ref, y)
        put(os_ref, as_[...] / ls[...])


def _split_rows(rows, lead):
    s = lambda i, n: rows[:, i * LANES:i * LANES + n]
    return {
        'fox_kv': s(0, 2 * HEAD_DIM).reshape(lead + (2, HEAD_DIM)),
        'fox_logf': s(1, N_HEADS).reshape(lead + (N_HEADS,)),
        'mla_ckv': s(2, MLA_KV_RANK).reshape(lead + (MLA_KV_RANK,)),
        'mla_krope': s(3, MLA_ROPE).reshape(lead + (MLA_ROPE,)),
        'diff_kv': s(4, 2 * HEAD_DIM).reshape(lead + (2, HEAD_DIM)),
        'dsa_kv': s(5, 2 * HEAD_DIM).reshape(lead + (2, HEAD_DIM)),
        'dsa_kidx': s(6, IDX_DIM).reshape(lead + (IDX_DIM,)),
    }


def _ffn_and_norm(l, routed, gate2, ln_g2, ln_b2, ew, rows_per_mod, alpha):
    x1, h2, ti, tg, rk, cnt = routed
    n_tok = x1.shape[0]
    counts = cnt[0, :N_EXPERTS].astype(I32)
    dest, src, blk_expert, n_blocks = _routing_tables(ti[:, :TOP_K], rk[:, :TOP_K], counts, n_tok)
    y_pad = _moe_blocks(l, h2, blk_expert, src, n_blocks, *ew)
    return _combine_final(x1, y_pad, dest, tg, gate2, ln_g2, ln_b2, rows_per_mod, alpha)


def _lambda(l, diff_lambda):
    lam_init = 0.8 - 0.6 * math.exp(-0.3 * l)
    dl = diff_lambda[l]
    lam = (jnp.exp(jnp.sum(dl[0] * dl[1]).astype(F32)) - jnp.exp(jnp.sum(dl[2] * dl[3]).astype(F32)) + lam_init)
    return lam, lam_init


def _decode_attention(l, pr, caches_t, page_table, t5_table, lam, lam_init, subln_lo, bd, dec):
    rows, cum, qf, qm, qd, qs, qi, iw = pr
    fox_t, logf_t, ckv_c, kr_t, diff_t, dsa_t, kidx_t = caches_t
    n_pages = page_table.shape[1]
    pages = min(PAGES_PER_STEP, n_pages)
    assert n_pages % pages == 0
    nch = n_pages // pages
    gk = pages * LANES
    nk_past = n_pages * LANES
    nsel = min(IDX_TOPK, (nk_past + dec) // 4)

    rows3 = rows.reshape(bd, dec, rows.shape[-1])

    def key_minor(slab, n):
        x = jnp.transpose(rows3[:, :, slab * LANES:slab * LANES + n], (0, 2, 1))
        return _pad_last(x, LANES)

    nf = key_minor(0, 2 * HEAD_DIM).reshape(bd, 2, HEAD_DIM, LANES)
    nl = key_minor(1, N_HEADS)
    nc = jnp.pad(rows3[:, :, 2 * LANES:3 * LANES], ((0, 0), (0, LANES - dec), (0, 0)))
    nkr = key_minor(3, MLA_ROPE)
    nd = key_minor(4, 2 * HEAD_DIM).reshape(bd, 2, HEAD_DIM, LANES)
    ns = key_minor(5, 2 * HEAD_DIM).reshape(bd, 2, HEAD_DIM, LANES)
    nidx = key_minor(6, IDX_DIM)
    cum3 = cum.reshape(bd, dec, LANES)[:, :, :N_HEADS]
    ecol = jnp.transpose(cum3[:, dec - 1:dec, :] - cum3, (0, 2, 1)).reshape(bd, N_HEADS * dec, 1)

    sem = ("arbitrary", "arbitrary")
    row_blk = lambda n: pl.BlockSpec((dec, n), lambda b, c, *_: (b, 0))
    per_b = lambda a: pl.BlockSpec((None,) + a.shape[1:], lambda b, c, *_: (b,) + (0,) * (a.ndim - 1))
    anyspec = pl.BlockSpec(memory_space=pl.ANY)

    gs_idx = pltpu.PrefetchScalarGridSpec(
        num_scalar_prefetch=1, grid=(bd, 1),
        in_specs=[anyspec, anyspec, row_blk(8 * LANES), row_blk(LANES), per_b(nidx)],
        out_specs=[pl.BlockSpec((None, dec, nk_past), lambda b, c, pt: (b, 0, 0)),
                   pl.BlockSpec((None, dec, LANES), lambda b, c, pt: (b, 0, 0)),
                   pl.BlockSpec((None, dec, LANES), lambda b, c, pt: (b, 0, 0)),
                   pl.BlockSpec((None, N_HEADS, n_pages, LANES), lambda b, c, pt: (b, 0, 0, 0))],
        scratch_shapes=[pltpu.VMEM((2, IDX_DIM, nk_past), F32), pltpu.VMEM((2, N_HEADS, n_pages, LANES), F32),
                        pltpu.SemaphoreType.DMA((2, 2)),
                        pltpu.VMEM((1, dec, nk_past), I32), pltpu.VMEM((dec, LANES), I32)])
    keys_past, keys_new, thr, after = pl.pallas_call(
        functools.partial(_dec_idx_kernel, layer=l, nch=1, pages=n_pages, dec=dec, nsel=nsel),
        grid_spec=gs_idx,
        out_shape=[jax.ShapeDtypeStruct((bd, dec, nk_past), I32), jax.ShapeDtypeStruct((bd, dec, LANES), I32),
                   jax.ShapeDtypeStruct((bd, dec, LANES), I32),
                   jax.ShapeDtypeStruct((bd, N_HEADS, n_pages, LANES), F32)],
        compiler_params=_cparams(sem), name="decode_indexer",
    )(page_table, kidx_t, logf_t, qi, iw, nidx)
    after = after.reshape(bd, N_HEADS, nk_past)

    tpos = jnp.arange(dec, dtype=I32)[:, None]
    rel_last = gk + tpos - jnp.arange(gk, dtype=I32)[None, :]
    rel_far = jnp.full((dec, gk), 2 * MAX_DISTANCE, I32)
    rel_new = tpos - jnp.arange(LANES, dtype=I32)[None, :]

    def tiles(cols, reps):
        tb = jnp.stack([_t5_rows(cols, rel_far, reps), _t5_rows(cols, rel_last, reps)], axis=0).astype(F32)
        return tb, _t5_rows(cols, rel_new, reps).astype(F32)

    tbd, tnd = tiles(t5_table[:, :N_HEADS], 2)
    tbs, tns = tiles(t5_table[:, N_HEADS:], 1)
    recent = lambda a: pl.BlockSpec((None,) + a.shape[1:], lambda b, c, *_: (jnp.where(c == 0, 1, 0), 0, 0))
    full = lambda a: pl.BlockSpec(a.shape, lambda b, c, *_: (0,) * a.ndim)
    nr = N_HEADS * dec
    col = lambda n: pltpu.VMEM((n, 1), F32)
    ospec = pl.BlockSpec((None, N_HEADS, dec, LANES), lambda b, c, *_: (b, 0, 0, 0))
    oshape = jax.ShapeDtypeStruct((bd, N_HEADS, dec, LANES), BF16)
    gs_att = pltpu.PrefetchScalarGridSpec(
        num_scalar_prefetch=1, grid=(bd, nch),
        in_specs=[pl.BlockSpec(memory_space=pltpu.SMEM)] + [anyspec] * 5 + [
            row_blk(4 * LANES), row_blk(8 * LANES), row_blk(8 * LANES), row_blk(4 * LANES), per_b(ecol),
            pl.BlockSpec((None, N_HEADS, gk), lambda b, c, pt: (b, 0, nch - 1 - c)),
            pl.BlockSpec((None, dec, gk), lambda b, c, pt: (b, 0, nch - 1 - c)), per_b(keys_new), per_b(thr),
            per_b(nf), per_b(nl), per_b(nc), per_b(nkr), per_b(nd), per_b(ns),
            recent(tbd), recent(tbs), full(tnd), full(tns), full(subln_lo)],
        out_specs=[ospec] * 4,
        scratch_shapes=[pltpu.VMEM((2, 2, HEAD_DIM, gk), F32),
                        pltpu.VMEM((2, gk, LANES), F32), pltpu.VMEM((2, MLA_ROPE, gk), F32),
                        pltpu.VMEM((2, 2, HEAD_DIM, gk), F32), pltpu.VMEM((2, 2, HEAD_DIM, gk), F32),
                        pltpu.SemaphoreType.DMA((2, 5)),
                        col(nr), col(nr), pltpu.VMEM((nr, HEAD_DIM), F32),
                        col(nr), col(nr), pltpu.VMEM((nr, LANES), F32),
                        col(2 * nr), col(2 * nr), pltpu.VMEM((2 * nr, HEAD_DIM), F32),
                        col(nr), col(nr), pltpu.VMEM((nr, HEAD_DIM), F32),
                        pltpu.VMEM((N_HEADS, 1), F32)])
    return pl.pallas_call(
        functools.partial(_dec_att_kernel, layer=l, nch=nch, pages=pages, dec=dec, lam_init=lam_init),
        grid_spec=gs_att,
        out_shape=[oshape] * 4,
        compiler_params=_cparams(sem), name="decode_attention",
    )(page_table, lam.reshape(1), fox_t, ckv_c, kr_t, diff_t, dsa_t,
      qf, qm, qd, qs, ecol, after, keys_past, keys_new, thr, nf, nl, nc, nkr, nd, ns, tbd, tbs, tnd, tns, subln_lo)


def _sample_layer(l, x, mod, wts, w_out_l, ln_g, ln_b, ew, t5_table, diff_lambda, diff_subln, alpha,
                  caches_t, page_table):
    bd, dec, d = x.shape
    past_len = page_table.shape[1] * LANES
    mod_rows = jnp.repeat(mod, dec, axis=0)
    m = [mod_rows[:, k * d:(k + 1) * d] for k in range(6)]
    cos_t, sin_t = _rope_tables(past_len + jnp.arange(dec, dtype=I32))
    cos_t, sin_t = jnp.tile(cos_t, (bd, 1)), jnp.tile(sin_t, (bd, 1))
    x2d = x.reshape(bd * dec, d)
    pr = _project(x2d, m[1], m[0], wts, cos_t, sin_t, dec, 1, False)
    lam, lam_init = _lambda(l, diff_lambda)
    o_list = _decode_attention(l, pr, caches_t, page_table, t5_table, lam, lam_init, diff_subln[l][None, :], bd, dec)
    wo = _prep_w_out(w_out_l, False)
    routed = _out_router(x2d, o_list, wo, wts['wv'], m[2], m[4], m[3], ln_g[l, 0:1], ln_b[l, 0:1],
                         wts['rw'], wts['rb'], 1, alpha)
    x2 = _ffn_and_norm(l, routed, m[5], ln_g[l, 1:2], ln_b[l, 1:2], ew, 1, alpha)
    return x2.reshape(bd, dec, d), _split_rows(pr[0], (bd, dec))


def _prompt_layer(l, x, mod, wts, w_out_l, ln_g, ln_b, ew, t5_table, diff_lambda, alpha):
    bsz, seq, d = x.shape
    m = [mod[:, None, k * d:(k + 1) * d] for k in range(6)]
    cos_t, sin_t = _rope_tables(jnp.arange(seq, dtype=I32))
    cos_t, sin_t = jnp.tile(cos_t, (bsz, 1)), jnp.tile(sin_t, (bsz, 1))
    x2d = x.reshape(bsz * seq, d)
    pr = _project(x2d, m[1], m[0], wts, cos_t, sin_t, seq, seq, True)
    lam, lam_init = _lambda(l, diff_lambda)
    o_list = _prompt_attention(pr, t5_table, lam, lam_init, wts['subln'], bsz, seq)
    wo = _prep_w_out(w_out_l, True)
    routed = _out_router(x2d, o_list, wo, wts['wv'], m[2], m[4], m[3], ln_g[l, 0:1], ln_b[l, 0:1],
                         wts['rw'], wts['rb'], seq, alpha)
    x2 = _ffn_and_norm(l, routed, m[5], ln_g[l, 1:2], ln_b[l, 1:2], ew, seq, alpha)
    return x2.reshape(bsz, seq, d), _split_rows(pr[0], (bsz, seq))


def kernel(x_prompt, x_sample, c_prompt, c_sample, cache_fox_kv, cache_fox_logf, cache_mla_ckv, cache_mla_krope,
           cache_diff_kv, cache_dsa_kv, cache_dsa_kidx, page_table, w_in, fox_f_bias, mla_q_norm, mla_w_uq,
           mla_kv_norm, mla_w_uk, mla_w_uv, diff_lambda, diff_subln, w_out, w_ada, b_ada, ln_g, ln_b,
           router_w, router_b, exp_w_gu, exp_b_gu, exp_w_dn, exp_b_dn, t5_table):
    depth = w_in.shape[0]
    alpha = (2 * depth) ** 0.25
    bsz, bd = x_prompt.shape[0], x_sample.shape[0]
    caches_t = (jnp.transpose(cache_fox_kv, (0, 1, 3, 4, 2)), jnp.transpose(cache_fox_logf, (0, 1, 3, 2)),
                cache_mla_ckv, jnp.transpose(cache_mla_krope, (0, 1, 3, 2)),
                jnp.transpose(cache_diff_kv, (0, 1, 3, 4, 2)), jnp.transpose(cache_dsa_kv, (0, 1, 3, 4, 2)),
                jnp.transpose(cache_dsa_kidx, (0, 1, 3, 2)))
    n_c = bsz + bd
    c_all = jnp.pad(jnp.concatenate([c_prompt, c_sample], axis=0), ((0, -n_c % 8), (0, 0)))
    mod_all = _ada(c_all, w_ada, b_ada)
    ew = (exp_w_gu, exp_b_gu, exp_w_dn, exp_b_dn)
    xp, xs = x_prompt, x_sample
    rows_p, rows_s = [], []
    for l in range(depth):
        wts = _prep_layer_weights(l, w_in, fox_f_bias, mla_q_norm, mla_w_uq, mla_kv_norm, mla_w_uk, mla_w_uv,
                                  diff_subln, w_out, router_w, router_b)
        xp, rp = _prompt_layer(l, xp, mod_all[l, :bsz], wts, w_out[l], ln_g, ln_b, ew, t5_table, diff_lambda, alpha)
        xs, rs = _sample_layer(l, xs, mod_all[l, bsz:n_c], wts, w_out[l], ln_g, ln_b, ew, t5_table, diff_lambda,
                               diff_subln, alpha, caches_t, page_table)
        rows_p.append(rp)
        rows_s.append(rs)

    def stack(rows, name):
        return jnp.stack([r[name] for r in rows], axis=0)

    out = [xp, xs]
    for name in ('fox_kv', 'fox_logf', 'mla_ckv', 'mla_krope', 'diff_kv', 'dsa_kv', 'dsa_kidx'):
        out += [stack(rows_p, name), stack(rows_s, name)]
    return tuple(out)
```

```python
import functools
import math

import numpy as np
import jax
import jax.numpy as jnp
from jax import lax
from jax.experimental import pallas as pl
from jax.experimental.pallas import tpu as pltpu

F32 = jnp.float32
BF16 = jnp.bfloat16
I32 = jnp.int32

LANES = 128
HEAD_DIM = 64
N_HEADS = 4
MLA_NOPE, MLA_ROPE, MLA_V, MLA_KV_RANK, MLA_Q_RANK = 64, 32, 64, 128, 192
DIFF_HALF = HEAD_DIM // 2
IDX_HEADS, IDX_DIM, IDX_TOPK = 8, 32, 256
ROPE_BASE = 10000.0
N_BUCKETS, MAX_DISTANCE = 32, 128
N_EXPERTS, TOP_K, EXPERT_BLOCK = 32, 4, 128
SWIGLU_LIMIT, SWIGLU_ALPHA = 7.0, 1.702
NEG_INF = -1e30
EPS = 1e-5
INT_MIN = -(2 ** 31)

IN_SIZES = (256, 64, 64, 4, 192, 128, 32, 256, 64, 64, 256, 64, 64, 256, 32, 8)
IN_OFFS = tuple(int(v) for v in np.cumsum((0,) + IN_SIZES))

S_FQ, S_FKV, S_FF, S_CQ, S_CKV, S_KR, S_DQ, S_DKV, S_SQ, S_SKV, S_IQ, S_IK, S_IW = (
    0, 4, 5, 6, 8, 9, 10, 18, 19, 23, 24, 32, 33)
N_SLABS = 34

ROW_TILE = 256
Q_TILE = 256
FINAL_TILE = 128
PAGES_PER_STEP = 32
VMEM_LIMIT = 56 * 1024 * 1024


def _cparams(sem, vmem=VMEM_LIMIT):
    return pltpu.CompilerParams(dimension_semantics=sem, vmem_limit_bytes=vmem)


def _bdot(a, b):
    return jnp.dot(a.astype(BF16), b.astype(BF16), preferred_element_type=F32)


def _bdot_nt(a, b):
    return lax.dot_general(a.astype(BF16), b.astype(BF16), (((1,), (1,)), ((), ())),
                           preferred_element_type=F32)


def _layer_norm(z, g, b):
    mu = jnp.mean(z, axis=-1, keepdims=True)
    var = jnp.mean(jnp.square(z - mu), axis=-1, keepdims=True)
    return (z - mu) * lax.rsqrt(var + EPS) * g + b


def _in_proj_columns():
    idx = np.full((N_SLABS * LANES,), -1, np.int64)

    def put(slab, lane0, src0, n):
        idx[slab * LANES + lane0: slab * LANES + lane0 + n] = np.arange(src0, src0 + n)

    o = IN_OFFS
    for h in range(N_HEADS):
        put(S_FQ + h, 0, o[0] + h * HEAD_DIM, HEAD_DIM)
        put(S_DQ + 2 * h, 0, o[7] + h * HEAD_DIM, DIFF_HALF)
        put(S_DQ + 2 * h + 1, DIFF_HALF, o[7] + h * HEAD_DIM + DIFF_HALF, DIFF_HALF)
        put(S_SQ + h, 0, o[10] + h * HEAD_DIM, HEAD_DIM)
    put(S_FKV, 0, o[1], 2 * HEAD_DIM)
    put(S_FF, 0, o[3], N_HEADS)
    put(S_CQ, 0, o[4], MLA_Q_RANK)
    put(S_CKV, 0, o[5], MLA_KV_RANK)
    put(S_KR, 0, o[6], MLA_ROPE)
    put(S_DKV, 0, o[8], 2 * HEAD_DIM)
    put(S_SKV, 0, o[11], 2 * HEAD_DIM)
    for h in range(IDX_HEADS):
        put(S_IQ + h, 0, o[13] + h * IDX_DIM, IDX_DIM)
    put(S_IK, 0, o[14], IDX_DIM)
    put(S_IW, 0, o[15], IDX_HEADS)
    return idx


def _column_runs(idx):
    runs, i = [], 0
    while i < len(idx):
        j = i + 1
        if idx[i] < 0:
            while j < len(idx) and idx[j] < 0:
                j += 1
            runs.append((0, j - i, True))
        else:
            while j < len(idx) and idx[j] == idx[j - 1] + 1:
                j += 1
            runs.append((int(idx[i]), j - i, False))
        i = j
    return tuple(runs)


_IN_RUNS = _column_runs(_in_proj_columns())


def _pad_last(a, n):
    return jnp.pad(a, [(0, 0)] * (a.ndim - 1) + [(0, n - a.shape[-1])])


def _prep_layer_weights(l, w_in, fox_f_bias, mla_q_norm, mla_w_uq, mla_kv_norm, mla_w_uk, mla_w_uv,
                        diff_subln, w_out, router_w, router_b):
    pieces = []
    for start, n, is_zero in _IN_RUNS:
        if is_zero:
            pieces.append(jnp.zeros((w_in.shape[1], n), w_in.dtype))
        else:
            pieces.append(w_in[l, :, start:start + n])
    w_in_p = jnp.concatenate(pieces, axis=1).astype(BF16)
    fb = _pad_last(fox_f_bias[l][None, :], LANES)
    gq = _pad_last(mla_q_norm[l][None, :], 2 * LANES)
    gkv = mla_kv_norm[l][None, :]
    wuq = mla_w_uq[l].reshape(MLA_Q_RANK, N_HEADS, MLA_NOPE + MLA_ROPE)
    nope = _pad_last(wuq[:, :, :MLA_NOPE], LANES).reshape(MLA_Q_RANK, N_HEADS * LANES)
    ropew = _pad_last(wuq[:, :, MLA_NOPE:], LANES).reshape(MLA_Q_RANK, N_HEADS * LANES)
    wuq_p = jnp.pad(jnp.concatenate([nope, ropew], axis=1), ((0, 2 * LANES - MLA_Q_RANK), (0, 0))).astype(BF16)
    wk = jnp.transpose(mla_w_uk[l], (1, 2, 0))
    wk = jnp.pad(wk, ((0, 0), (0, LANES - MLA_NOPE), (0, 0))).astype(BF16)
    wv = jnp.transpose(mla_w_uv[l], (1, 0, 2))
    wv = _pad_last(wv, LANES).astype(BF16)
    subln = jnp.pad(diff_subln[l][None, :], ((0, 0), (HEAD_DIM, 0)))
    rw = _pad_last(router_w[l], LANES)
    rb = jnp.pad(router_b[l][None, :], ((0, 0), (0, LANES - N_EXPERTS)), constant_values=-3e38)
    return dict(w_in_p=w_in_p, fb=fb, gq=gq, gkv=gkv, wuq_p=wuq_p, wk=wk, wv=wv, subln=subln, rw=rw, rb=rb)


def _prep_w_out(w_out_l, value_in_upper_half):
    w = w_out_l.reshape(4, N_HEADS, HEAD_DIM, -1)
    lo = jnp.pad(w, ((0, 0), (0, 0), (0, HEAD_DIM), (0, 0)))
    hi = jnp.pad(w, ((0, 0), (0, 0), (HEAD_DIM, 0), (0, 0)))
    sel = hi if value_in_upper_half else lo
    out = jnp.stack([sel[0], lo[1], sel[2], sel[3]], axis=0)
    return out.reshape(16, LANES, -1).astype(BF16)


def _t5_bucket(rel):
    n = jnp.maximum(rel, 0)
    max_exact = N_BUCKETS // 2
    nf = jnp.maximum(n, 1).astype(F32)
    large = max_exact + (jnp.log(nf / max_exact) / math.log(MAX_DISTANCE / max_exact)
                         * (N_BUCKETS - max_exact)).astype(I32)
    return jnp.where(n < max_exact, n, jnp.minimum(large, N_BUCKETS - 1))


def _t5_rows(table_cols, rel, reps):
    hot = (_t5_bucket(rel)[..., None] == jnp.arange(N_BUCKETS, dtype=I32)).astype(F32)
    b = jnp.einsum('rck,kh->hrc', hot, table_cols.astype(F32), precision=lax.Precision.HIGHEST)
    b = b.reshape(N_HEADS * rel.shape[0], rel.shape[1])
    return jnp.concatenate([b] * reps, axis=0)


def _rope_tables(pos):
    half = MLA_ROPE // 2
    inv_freq = ROPE_BASE ** (-jnp.arange(half, dtype=F32) / half)
    ang = pos.astype(F32)[:, None] * inv_freq[None, :]
    cos, sin = jnp.cos(ang), jnp.sin(ang)
    cos_t = _pad_last(jnp.concatenate([cos, cos], axis=1), LANES)
    sin_t = _pad_last(jnp.concatenate([-sin, sin], axis=1), LANES)
    return cos_t, sin_t


def _ada_kernel(c_ref, w_ref, b_ref, o_ref):
    c = c_ref[...]
    o_ref[0] = _bdot(c * jax.nn.sigmoid(c), w_ref[0]) + b_ref[0]


def _ada(c_all, w_ada, b_ada):
    depth, d, n = w_ada.shape
    rows = c_all.shape[0]
    tn = n // 4 if n % (4 * LANES) == 0 else n
    return pl.pallas_call(
        _ada_kernel,
        grid=(depth, n // tn),
        in_specs=[pl.BlockSpec((rows, d), lambda l, j: (0, 0)),
                  pl.BlockSpec((1, d, tn), lambda l, j: (l, 0, j)),
                  pl.BlockSpec((1, 1, tn), lambda l, j: (l, 0, j))],
        out_specs=pl.BlockSpec((1, rows, tn), lambda l, j: (l, 0, j)),
        out_shape=jax.ShapeDtypeStruct((depth, rows, n), F32),
        compiler_params=_cparams(("arbitrary", "arbitrary")),
        name="ada_mod",
    )(c_all, w_ada, b_ada.reshape(depth, 1, n))


def _rope_apply(x, cos_t, sin_t):
    lane = lax.broadcasted_iota(I32, x.shape, 1)
    first_half = (lane % MLA_ROPE) < (MLA_ROPE // 2)
    swapped = jnp.where(first_half, pltpu.roll(x, LANES - MLA_ROPE // 2, 1), pltpu.roll(x, MLA_ROPE // 2, 1))
    return x * cos_t + swapped * sin_t


def _proj_kernel(*refs, seq, tr, transposed):
    (x_ref, sc_ref, sh_ref, w_ref, fb_ref, gq_ref, wuq_ref, gkv_ref, wk_ref, cos_ref, sin_ref) = refs[:11]
    out_refs, carry_ref = refs[11:-1], refs[-1]
    r = pl.program_id(0)
    h = x_ref[...] * (1.0 + sc_ref[0]) + sh_ref[0]
    p = _bdot(h, w_ref[...])

    def slab(i, n=1):
        return p[:, i * LANES:(i + n) * LANES]

    lane = lax.broadcasted_iota(I32, (tr, LANES), 1)
    ff = slab(S_FF) + fb_ref[...]
    logf = jnp.minimum(ff, 0.0) - jnp.log(1.0 + jnp.exp(-jnp.abs(ff)))
    logf = jnp.where(lane < N_HEADS, logf, 0.0)

    cq = slab(S_CQ, 2)
    cqn = cq * lax.rsqrt(jnp.sum(cq * cq, axis=-1, keepdims=True) * (1.0 / MLA_Q_RANK) + EPS) * gq_ref[...]
    q_mla = _bdot(cqn, wuq_ref[...])
    cos_t, sin_t = cos_ref[...], sin_ref[...]
    qm_slabs = []
    for hd in range(N_HEADS):
        q_nope = q_mla[:, hd * LANES:(hd + 1) * LANES]
        qm_slabs.append(_bdot(q_nope, wk_ref[hd]))
        q_rope = q_mla[:, (N_HEADS + hd) * LANES:(N_HEADS + hd + 1) * LANES]
        qm_slabs.append(_rope_apply(q_rope, cos_t, sin_t))

    ckv = slab(S_CKV)
    ckvn = ckv * lax.rsqrt(jnp.mean(ckv * ckv, axis=-1, keepdims=True) + EPS) * gkv_ref[...]
    krope = _rope_apply(slab(S_KR), cos_t, sin_t)

    rows_ref = out_refs[0]
    for i, v in enumerate((slab(S_FKV), logf, ckvn, krope, slab(S_DKV), slab(S_SKV), slab(S_IK))):
        rows_ref[:, i * LANES:(i + 1) * LANES] = v

    ti = lax.broadcasted_iota(I32, (tr, tr), 0)
    tj = lax.broadcasted_iota(I32, (tr, tr), 1)
    if tr <= seq:
        tri = tj <= ti
    else:
        tri = (tj <= ti) & ((tj // seq) == (ti // seq))
    cum = jnp.dot(tri.astype(F32), logf, precision=lax.Precision.HIGHEST, preferred_element_type=F32)
    if tr <= seq:
        @pl.when(r % (seq // tr) == 0)
        def _():
            carry_ref[...] = jnp.zeros_like(carry_ref)
        cum = cum + carry_ref[0:1, :]
        carry_ref[...] = jnp.broadcast_to(cum[tr - 1:tr, :], carry_ref.shape)
    out_refs[1][...] = cum

    if not transposed:
        qf_ref, qm_ref, qd_ref, qs_ref, qi_ref, iw_ref = out_refs[2:]
        qf_ref[...] = slab(S_FQ, N_HEADS).astype(BF16)
        for i, v in enumerate(qm_slabs):
            qm_ref[:, i * LANES:(i + 1) * LANES] = v.astype(BF16)
        qd_ref[...] = slab(S_DQ, 2 * N_HEADS).astype(BF16)
        qs_ref[...] = slab(S_SQ, N_HEADS).astype(BF16)
        qi_ref[...] = slab(S_IQ, IDX_HEADS).astype(BF16)
        iw_ref[...] = slab(S_IW)
        return

    kvb_ref, vt_ref, qt_ref, aux_ref = out_refs[2:]
    hi = cum.astype(BF16).astype(F32)
    mid = (cum - hi).astype(BF16).astype(F32)
    lo = cum - hi - mid
    ones = jnp.where((lane >= 3 * N_HEADS) & (lane < 3 * N_HEADS + 3), 1.0, 0.0)
    aug = hi + pltpu.roll(mid, N_HEADS, 1) + pltpu.roll(lo, 2 * N_HEADS, 1) + ones
    for i, v in enumerate((ckvn, krope, slab(S_FKV), aug, slab(S_DKV), aug, slab(S_SKV), slab(S_IK))):
        kvb_ref[:, i * LANES:(i + 1) * LANES] = v.astype(BF16)
    for i, v in enumerate((ckvn, slab(S_FKV), slab(S_DKV), slab(S_SKV))):
        vt_ref[0, i * LANES:(i + 1) * LANES, :] = v.T.astype(BF16)
    q_list = ([v * (MLA_NOPE + MLA_ROPE) ** -0.5 for v in qm_slabs]
              + [slab(S_DQ + i) * DIFF_HALF ** -0.5 for i in range(2 * N_HEADS)]
              + [slab(S_IQ + i) for i in range(IDX_HEADS)]
              + [slab(S_FQ + i) * HEAD_DIM ** -0.5 for i in range(N_HEADS)]
              + [slab(S_SQ + i) * HEAD_DIM ** -0.5 for i in range(N_HEADS)])
    for i, v in enumerate(q_list):
        qt_ref[0, i * LANES:(i + 1) * LANES, :] = v.T.astype(BF16)
    aux_ref[0, 0:LANES, :] = cum.T
    aux_ref[0, LANES:2 * LANES, :] = slab(S_IW).T


def _tile_mod(a, tr):
    return a if a.ndim == 3 else a.reshape(a.shape[0] // tr, tr, a.shape[1])


def _mod_map(mr, tr, rows_per_mod):
    if mr == 1:
        return lambda r, *_: (r * tr // rows_per_mod, 0, 0)
    return lambda r, *_: (r, 0, 0)


def _project(x2d, scale_b, shift_b, wts, cos_t, sin_t, seq, rows_per_mod, transposed):
    rtot, d = x2d.shape
    tr = min(ROW_TILE, rtot)
    assert rtot % tr == 0 and (seq % tr == 0 or tr % seq == 0)
    scale_b, shift_b = _tile_mod(scale_b, tr), _tile_mod(shift_b, tr)
    mr = scale_b.shape[1]
    mod_map = _mod_map(mr, tr, rows_per_mod)
    full = lambda a: pl.BlockSpec(a.shape, lambda r: (0,) * a.ndim)
    row = lambda n: pl.BlockSpec((tr, n), lambda r: (r, 0))
    outs = [(7 * LANES, F32), (LANES, F32)]
    if transposed:
        outs += [(8 * LANES, BF16)]
        touts = [(4 * LANES, BF16), (32 * LANES, BF16), (2 * LANES, F32)]
    else:
        outs += [(4 * LANES, BF16), (8 * LANES, BF16), (8 * LANES, BF16), (4 * LANES, BF16), (8 * LANES, BF16),
                 (LANES, F32)]
        touts = []
    return pl.pallas_call(
        functools.partial(_proj_kernel, seq=seq, tr=tr, transposed=transposed),
        grid=(rtot // tr,),
        in_specs=[row(d), pl.BlockSpec((1, mr, d), mod_map), pl.BlockSpec((1, mr, d), mod_map),
                  full(wts['w_in_p']), full(wts['fb']), full(wts['gq']), full(wts['wuq_p']), full(wts['gkv']),
                  full(wts['wk']), row(LANES), row(LANES)],
        out_specs=[row(n) for n, _ in outs] + [pl.BlockSpec((1, n, tr), lambda r: (r, 0, 0)) for n, _ in touts],
        out_shape=[jax.ShapeDtypeStruct((rtot, n), dt) for n, dt in outs]
        + [jax.ShapeDtypeStruct((rtot // tr, n, tr), dt) for n, dt in touts],
        scratch_shapes=[pltpu.VMEM((8, LANES), F32)],
        compiler_params=_cparams(("arbitrary",)),
        name="in_proj",
    )(x2d, scale_b, shift_b, wts['w_in_p'], wts['fb'], wts['gq'], wts['wuq_p'], wts['gkv'], wts['wk'],
      cos_t, sin_t)


def _col_online(st, vt, m_ref, l_ref, acc_ref, keep=None):
    m_prev = m_ref[...]
    m_new = jnp.maximum(m_prev, jnp.max(st, axis=0, keepdims=True))
    alpha = jnp.exp(m_prev - m_new)
    p = jnp.exp(st - m_new)
    if keep is not None:
        p = jnp.where(keep, p, 0.0)
    l_ref[...] = alpha * l_ref[...] + jnp.sum(p, axis=0, keepdims=True)
    acc_ref[...] = alpha * acc_ref[...] + jnp.dot(vt, p.astype(BF16), preferred_element_type=F32)
    m_ref[...] = m_new


def _causal_t(tk, n, tq):
    si = lax.broadcasted_iota(I32, (tk, n), 0)
    ti = lax.broadcasted_iota(I32, (tk, n), 1) % tq
    return si <= ti


def _store_heads_t(o_ref, ot, tq):
    for h in range(N_HEADS):
        o_ref[h] = ot[:, h * tq:(h + 1) * tq].T.astype(BF16)


def _init_state(m_ref, l_ref, acc_ref):
    m_ref[...] = jnp.full(m_ref.shape, NEG_INF, F32)
    l_ref[...] = jnp.zeros(l_ref.shape, F32)
    acc_ref[...] = jnp.zeros(acc_ref.shape, F32)


def _split3(x):
    hi = x.astype(BF16).astype(F32)
    mid = (x - hi).astype(BF16).astype(F32)
    return hi, mid, x - hi - mid


def _const_rows(row):
    hi, mid, lo = _split3(row)
    ri = lax.broadcasted_iota(I32, (LANES, row.shape[1]), 0)
    blk = jnp.where(ri == 3 * N_HEADS, hi, 0.0)
    blk = jnp.where(ri == 3 * N_HEADS + 1, mid, blk)
    blk = jnp.where(ri == 3 * N_HEADS + 2, lo, blk)
    return blk.astype(BF16)


def _fox_kernel(qt_ref, aux_ref, k_ref, vt_ref, o_ref, qs_ref, m_ref, l_ref, acc_ref, *, tq):
    i = pl.program_id(1)
    ri = lax.broadcasted_iota(I32, (LANES, tq), 0)
    for h in range(N_HEADS):
        qs_ref[0:LANES, h * tq:(h + 1) * tq] = qt_ref[h * LANES:(h + 1) * LANES, :]
        hi, mid, lo = _split3(aux_ref[h:h + 1, :])
        blk = jnp.where((ri < 3 * N_HEADS) & (ri % N_HEADS == h), -1.0, 0.0)
        blk = jnp.where(ri == 3 * N_HEADS, hi, blk)
        blk = jnp.where(ri == 3 * N_HEADS + 1, mid, blk)
        blk = jnp.where(ri == 3 * N_HEADS + 2, lo, blk)
        qs_ref[LANES:2 * LANES, h * tq:(h + 1) * tq] = blk.astype(BF16)
    _init_state(m_ref, l_ref, acc_ref)

    def step(j, diag):
        k = k_ref[pl.ds(pl.multiple_of(j * tq, tq), tq), :]
        st = jnp.dot(k, qs_ref[...], preferred_element_type=F32)
        if diag:
            st = jnp.where(_causal_t(tq, N_HEADS * tq, tq), st, NEG_INF)
        _col_online(st, vt_ref[j], m_ref, l_ref, acc_ref)

    def body(j, c):
        step(j, False)
        return c

    lax.fori_loop(0, i, body, 0)
    step(i, True)
    _store_heads_t(o_ref, acc_ref[...] / l_ref[...], tq)


def _mla_kernel(qt_ref, k_ref, vt_ref, o_ref, qs_ref, m_ref, l_ref, acc_ref, *, tq):
    i = pl.program_id(1)
    for h in range(N_HEADS):
        qs_ref[:, h * tq:(h + 1) * tq] = qt_ref[h * 2 * LANES:(h + 1) * 2 * LANES, :]
    _init_state(m_ref, l_ref, acc_ref)

    def step(j, diag):
        k = k_ref[pl.ds(pl.multiple_of(j * tq, tq), tq), :]
        st = jnp.dot(k, qs_ref[...], preferred_element_type=F32)
        if diag:
            st = jnp.where(_causal_t(tq, N_HEADS * tq, tq), st, NEG_INF)
        _col_online(st, vt_ref[j], m_ref, l_ref, acc_ref)

    def body(j, c):
        step(j, False)
        return c

    lax.fori_loop(0, i, body, 0)
    step(i, True)
    _store_heads_t(o_ref, acc_ref[...] / l_ref[...], tq)


def _near_far_loop(i, step):
    def body(j, c):
        step(j, 2)
        return c

    lax.fori_loop(0, jnp.maximum(i - 1, 0), body, 0)

    @pl.when(i >= 1)
    def _():
        step(i - 1, 1)

    step(i, 0)


def _diff_kernel(lam_ref, qt_ref, k_ref, vt_ref, tb_ref, tfar_ref, g_ref, o_ref, qs_ref, m_ref, l_ref, acc_ref,
                 *, tq, lam_init):
    i = pl.program_id(1)
    nq = N_HEADS * tq
    for g in range(2):
        for h in range(N_HEADS):
            qs_ref[0:LANES, (g * N_HEADS + h) * tq:(g * N_HEADS + h + 1) * tq] = \
                qt_ref[(2 * h + g) * LANES:(2 * h + g + 1) * LANES, :]
    qs_ref[LANES:2 * LANES, :] = _const_rows(tfar_ref[...])
    _init_state(m_ref, l_ref, acc_ref)

    def step(j, dist):
        k = k_ref[pl.ds(pl.multiple_of(j * tq, tq), tq), :]
        st = jnp.dot(k, qs_ref[...], preferred_element_type=F32)
        if dist < 2:
            st = st + tb_ref[dist]
        if dist == 0:
            st = jnp.where(_causal_t(tq, 2 * nq, tq), st, NEG_INF)
        _col_online(st, vt_ref[j], m_ref, l_ref, acc_ref)

    _near_far_loop(i, step)
    ot = acc_ref[...] / l_ref[...]
    ot = ot[:, :nq] - lam_ref[0] * ot[:, nq:]
    ri = lax.broadcasted_iota(I32, ot.shape, 0)
    ot = jnp.where(ri >= HEAD_DIM, ot, 0.0)
    y = ot * lax.rsqrt(jnp.sum(ot * ot, axis=0, keepdims=True) * (1.0 / HEAD_DIM) + EPS) * g_ref[...]
    _store_heads_t(o_ref, y * (1.0 - lam_init), tq)


def _score_keys(sc):
    bits = lax.bitcast_convert_type(sc, I32)
    key = jnp.where(bits < 0, bits ^ jnp.int32(0x7FFFFFFF), bits)
    return jnp.where(sc == 0.0, 0, key)


def _dsa_kernel(qit_ref, iwt_ref, ki_ref, qt_ref, k_ref, vt_ref, tb_ref, tfar_ref, o_ref,
                keys_ref, qis_ref, wr_ref, thr_ref, qs_ref, m_ref, l_ref, acc_ref, *, tq, nsel):
    i = pl.program_id(1)
    for h in range(IDX_HEADS):
        qis_ref[:, h * tq:(h + 1) * tq] = qit_ref[h * LANES:(h + 1) * LANES, :]
        wr_ref[:, h * tq:(h + 1) * tq] = iwt_ref[h:h + 1, :] * (IDX_HEADS ** -0.5 * IDX_DIM ** -0.5)
    for h in range(N_HEADS):
        qs_ref[:, h * tq:(h + 1) * tq] = qt_ref[h * LANES:(h + 1) * LANES, :]
    _init_state(m_ref, l_ref, acc_ref)

    def score_tile(j, diag):
        kt = ki_ref[pl.ds(pl.multiple_of(j * tq, tq), tq), :]
        r = jnp.maximum(jnp.dot(kt, qis_ref[...], preferred_element_type=F32), 0.0) * wr_ref[...]
        sc = r[:, 0:tq]
        for h in range(1, IDX_HEADS):
            sc = sc + r[:, h * tq:(h + 1) * tq]
        key = _score_keys(sc)
        if diag:
            key = jnp.where(_causal_t(tq, tq, tq), key, INT_MIN)
        keys_ref[j] = key

    def sbody(j, c):
        score_tile(j, False)
        return c

    lax.fori_loop(0, i, sbody, 0)
    score_tile(i, True)

    def count_ge(cand):
        def cbody(j, c):
            ge = jnp.where(keys_ref[j] >= cand, 1.0, 0.0)
            return c + jnp.sum(ge.reshape(tq // 8, 8, tq), axis=0)

        c = lax.fori_loop(0, i + 1, cbody, jnp.zeros((8, tq), F32))
        return jnp.sum(c, axis=0, keepdims=True)

    zero = jnp.zeros((1, tq), I32)
    t0 = jnp.where(count_ge(zero) >= nsel, zero, INT_MIN)

    def bit_body(b, t):
        cand = t + jnp.left_shift(jnp.int32(1), 30 - b)
        return jnp.where(count_ge(cand) >= nsel, cand, t)

    thr_ref[...] = lax.fori_loop(0, 31, bit_body, t0)

    def step(j, dist):
        k = k_ref[pl.ds(pl.multiple_of(j * tq, tq), tq), :]
        st = jnp.dot(k, qs_ref[...], preferred_element_type=F32) + tfar_ref[...]
        if dist < 2:
            st = st + tb_ref[dist]
        sel = keys_ref[j] >= thr_ref[...]
        if dist == 0:
            sel = sel & _causal_t(tq, tq, tq)
        self32 = jnp.where(sel, 1.0, 0.0)
        keep = jnp.concatenate([self32] * N_HEADS, axis=1) > 0.5
        st = jnp.where(keep, st, NEG_INF)
        _col_online(st, vt_ref[j], m_ref, l_ref, acc_ref, keep=keep)

    _near_far_loop(i, step)
    _store_heads_t(o_ref, acc_ref[...] / l_ref[...], tq)


def _prompt_attention(pr, t5_table, lam, lam_init, subln, bsz, seq):
    _, _, kvb, vt, qt, aux = pr
    tq = min(Q_TILE, seq)
    assert seq % tq == 0 and tq == min(ROW_TILE, bsz * seq)
    nt = seq // tq
    kvb3 = kvb.reshape(bsz, seq, kvb.shape[-1])
    vt4 = vt.reshape(bsz, nt, vt.shape[1], tq)

    qspec = lambda n, blk: pl.BlockSpec((None, n * LANES, tq), lambda b, i: (b * nt + i, blk, 0))
    auxspec = lambda blk: pl.BlockSpec((None, LANES, tq), lambda b, i: (b * nt + i, blk, 0))
    kspec = lambda n, blk: pl.BlockSpec((None, seq, n * LANES), lambda b, i: (b, 0, blk))
    vspec = lambda blk: pl.BlockSpec((None, nt, LANES, tq), lambda b, i: (b, 0, blk, 0))
    ospec = pl.BlockSpec((None, N_HEADS, tq, LANES), lambda b, i: (b, 0, i, 0))
    oshape = jax.ShapeDtypeStruct((bsz, N_HEADS, seq, LANES), BF16)
    rowv = lambda n: pltpu.VMEM((1, n), F32)
    grid = (bsz, nt)
    sem = ("arbitrary", "arbitrary")
    nq = N_HEADS * tq

    o_fox = pl.pallas_call(
        functools.partial(_fox_kernel, tq=tq), grid=grid,
        in_specs=[qspec(4, 6), auxspec(0), kspec(2, 1), vspec(1)],
        out_specs=ospec, out_shape=oshape,
        scratch_shapes=[pltpu.VMEM((2 * LANES, nq), BF16), rowv(nq), rowv(nq), pltpu.VMEM((LANES, nq), F32)],
        compiler_params=_cparams(sem), name="fox_attention",
    )(qt, aux, kvb3, vt4)

    o_mla = pl.pallas_call(
        functools.partial(_mla_kernel, tq=tq), grid=grid,
        in_specs=[qspec(8, 0), kspec(2, 0), vspec(0)],
        out_specs=ospec, out_shape=oshape,
        scratch_shapes=[pltpu.VMEM((2 * LANES, nq), BF16), rowv(nq), rowv(nq), pltpu.VMEM((LANES, nq), F32)],
        compiler_params=_cparams(sem), name="mla_attention",
    )(qt, kvb3, vt4)

    ri = jnp.arange(tq, dtype=I32)
    rel0 = ri[:, None] - ri[None, :]
    far_rel = jnp.full((1, 1), 2 * MAX_DISTANCE, I32)

    def t5_tiles(cols, reps):
        tb = jnp.stack([_t5_rows(cols, rel0, reps).T, _t5_rows(cols, rel0 + tq, reps).T], axis=0)
        tfar = jnp.repeat(_t5_rows(cols, far_rel, reps), tq, axis=0).T
        return (tb - tfar[None]).astype(F32), tfar.astype(F32)

    assert tq >= MAX_DISTANCE
    tb_d, tfar_d = t5_tiles(t5_table[:, :N_HEADS], 2)
    tb_s, tfar_s = t5_tiles(t5_table[:, N_HEADS:], 1)
    full = lambda a: pl.BlockSpec(a.shape, lambda b, i: (0,) * a.ndim)
    gcol = subln.T

    o_diff = pl.pallas_call(
        functools.partial(_diff_kernel, tq=tq, lam_init=lam_init), grid=grid,
        in_specs=[pl.BlockSpec(memory_space=pltpu.SMEM), qspec(8, 1), kspec(2, 2), vspec(2),
                  full(tb_d), full(tfar_d), full(gcol)],
        out_specs=ospec, out_shape=oshape,
        scratch_shapes=[pltpu.VMEM((2 * LANES, 2 * nq), BF16), rowv(2 * nq), rowv(2 * nq),
                        pltpu.VMEM((LANES, 2 * nq), F32)],
        compiler_params=_cparams(sem), name="diff_attention",
    )(lam.reshape(1), qt, kvb3, vt4, tb_d, tfar_d, gcol)

    nsel = min(IDX_TOPK, seq // 4)
    o_dsa = pl.pallas_call(
        functools.partial(_dsa_kernel, tq=tq, nsel=nsel), grid=grid,
        in_specs=[qspec(8, 2), auxspec(1), kspec(1, 7), qspec(4, 7), kspec(1, 6), vspec(3),
                  full(tb_s), full(tfar_s)],
        out_specs=ospec, out_shape=oshape,
        scratch_shapes=[pltpu.VMEM((nt, tq, tq), I32), pltpu.VMEM((LANES, IDX_HEADS * tq), BF16),
                        rowv(IDX_HEADS * tq), pltpu.VMEM((1, tq), I32),
                        pltpu.VMEM((LANES, nq), BF16), rowv(nq), rowv(nq), pltpu.VMEM((LANES, nq), F32)],
        compiler_params=_cparams(sem), name="dsa_attention",
    )(qt, aux, kvb3, qt, kvb3, vt4, tb_s, tfar_s)
    return o_fox, o_mla, o_diff, o_dsa


def _out_kernel(x_ref, of_ref, om_ref, od_ref, os_ref, wo_ref, wv_ref, g1_ref, sc2_ref, sh2_ref,
                lng_ref, lnb_ref, rw_ref, rb_ref, x1_ref, h2_ref, ti_ref, tg_ref, rk_ref, cnt_ref, cnt_sc,
                *, tr, alpha):
    d = x_ref.shape[-1]
    acc = jnp.zeros((tr, d), F32)
    for h in range(N_HEADS):
        acc = acc + _bdot(of_ref[:, h].reshape(tr, LANES), wo_ref[h])
        o_mla = _bdot(om_ref[:, h].reshape(tr, LANES), wv_ref[h])
        acc = acc + _bdot(o_mla, wo_ref[N_HEADS + h])
        acc = acc + _bdot(od_ref[:, h].reshape(tr, LANES), wo_ref[2 * N_HEADS + h])
        acc = acc + _bdot(os_ref[:, h].reshape(tr, LANES), wo_ref[3 * N_HEADS + h])
    x1 = _layer_norm(alpha * x_ref[...] + g1_ref[0] * acc, lng_ref[...], lnb_ref[...])
    x1_ref[...] = x1
    h2 = x1 * (1.0 + sc2_ref[0]) + sh2_ref[0]
    h2_ref[...] = h2
    logits = jnp.dot(h2, rw_ref[...], precision=lax.Precision.HIGHEST, preferred_element_type=F32) + rb_ref[...]
    lane = lax.broadcasted_iota(I32, (tr, LANES), 1).astype(F32)
    idx_out = jnp.zeros((tr, LANES), F32)
    vals, hots = [], []
    for k in range(TOP_K):
        mx = jnp.max(logits, axis=1, keepdims=True)
        ix = jnp.min(jnp.where(logits == mx, lane, float(LANES)), axis=1, keepdims=True)
        idx_out = jnp.where(lane == k, ix, idx_out)
        vals.append(mx)
        hots.append(lane == ix)
        logits = jnp.where(hots[-1], -3.4e38, logits)
    @pl.when(pl.program_id(0) == 0)
    def _():
        cnt_sc[...] = jnp.zeros_like(cnt_sc)
    picked = sum(jnp.where(hh, 1.0, 0.0) for hh in hots)
    ti_ = lax.broadcasted_iota(I32, (tr, tr), 0)
    tj_ = lax.broadcasted_iota(I32, (tr, tr), 1)
    before = _bdot(jnp.where(tj_ < ti_, 1.0, 0.0), picked) + cnt_sc[0:1, :]
    rank_out = jnp.zeros((tr, LANES), F32)
    for k in range(TOP_K):
        rk = jnp.sum(jnp.where(hots[k], before, 0.0), axis=1, keepdims=True)
        rank_out = jnp.where(lane == k, rk, rank_out)
    rk_ref[...] = rank_out.astype(I32)
    cnt_new = cnt_sc[...] + jnp.sum(picked, axis=0, keepdims=True)
    cnt_sc[...] = cnt_new
    cnt_ref[...] = cnt_new
    es = [jnp.exp(v - vals[0]) for v in vals]
    den = es[0] + es[1] + es[2] + es[3]
    gates = jnp.zeros((tr, LANES), F32)
    for k in range(TOP_K):
        gates = jnp.where(lane == k, es[k] / den, gates)
    ti_ref[...] = idx_out.astype(I32)
    tg_ref[...] = gates


def _out_router(x2d, o_list, wo, wv, gate1, scale2, shift2, ln_g, ln_b, rw, rb, rows_per_mod, alpha):
    rtot, d = x2d.shape
    tr = min(ROW_TILE, rtot)
    nb_o, _, tt, _ = o_list[0].shape
    gate1, scale2, shift2 = _tile_mod(gate1, tr), _tile_mod(scale2, tr), _tile_mod(shift2, tr)
    mr = gate1.shape[1]
    mod_map = _mod_map(mr, tr, rows_per_mod)
    if tt >= tr:
        assert tt % tr == 0
        per = tt // tr
        ospec = pl.BlockSpec((1, N_HEADS, tr, LANES), lambda r: (r // per, 0, r % per, 0))
    else:
        assert tr % tt == 0
        ospec = pl.BlockSpec((tr // tt, N_HEADS, tt, LANES), lambda r: (r, 0, 0, 0))
    full = lambda a: pl.BlockSpec(a.shape, lambda r: (0,) * a.ndim)
    row = lambda n: pl.BlockSpec((tr, n), lambda r: (r, 0))
    mspec = pl.BlockSpec((1, mr, d), mod_map)
    return pl.pallas_call(
        functools.partial(_out_kernel, tr=tr, alpha=alpha),
        grid=(rtot // tr,),
        in_specs=[row(d), ospec, ospec, ospec, ospec, full(wo), full(wv), mspec, mspec, mspec,
                  full(ln_g), full(ln_b), full(rw), full(rb)],
        out_specs=[row(d), row(d), row(LANES), row(LANES), row(LANES), pl.BlockSpec((8, LANES), lambda r: (0, 0))],
        out_shape=[jax.ShapeDtypeStruct((rtot, d), F32), jax.ShapeDtypeStruct((rtot, d), F32),
                   jax.ShapeDtypeStruct((rtot, LANES), I32), jax.ShapeDtypeStruct((rtot, LANES), F32),
                   jax.ShapeDtypeStruct((rtot, LANES), I32), jax.ShapeDtypeStruct((8, LANES), F32)],
        scratch_shapes=[pltpu.VMEM((8, LANES), F32)],
        compiler_params=_cparams(("arbitrary",)),
        name="out_proj_router",
    )(x2d, *o_list, wo, wv, gate1, scale2, shift2, ln_g, ln_b, rw, rb)


def _routing_tables(top_idx, rank, counts, n_tok):
    a = n_tok * TOP_K
    e_flat = top_idx.reshape(a)
    padded = (counts + EXPERT_BLOCK - 1) // EXPERT_BLOCK * EXPERT_BLOCK
    pstart = jnp.cumsum(padded) - padded
    dest = (pstart[e_flat] + rank.reshape(a)).astype(I32)
    n_blocks = -(-a // EXPERT_BLOCK) + N_EXPERTS
    src = jnp.zeros((n_blocks * EXPERT_BLOCK,), I32).at[dest].set(jnp.arange(a, dtype=I32) // TOP_K)
    ends = pstart + padded
    first_row = jnp.arange(n_blocks, dtype=I32) * EXPERT_BLOCK
    blk_expert = jnp.minimum(jnp.sum((ends[None, :] <= first_row[:, None]).astype(I32), axis=1), N_EXPERTS - 1)
    return dest, src, blk_expert, n_blocks


def _moe_kernel(be_ref, src_ref, h_hbm, wgu_ref, bgu_ref, wdn_ref, bdn_ref, y_ref,
                xbuf0, xbuf1, sem, wgu_b, wdn_b, *, nblk, dff):
    i = pl.program_id(0)
    slot = i % 2

    bufs = (xbuf0, xbuf1)

    def row_copy(blk, sl, r):
        tok = src_ref[blk * EXPERT_BLOCK + r]
        return pltpu.make_async_copy(h_hbm.at[pl.ds(tok, 1)], bufs[sl].at[pl.ds(r, 1)], sem.at[sl])

    def wait_rows(blk, sl):
        del blk
        pltpu.make_async_copy(h_hbm.at[pl.ds(0, EXPERT_BLOCK)], bufs[sl], sem.at[sl]).wait()

    @pl.when(i == 0)
    def _():
        def body(r, c):
            row_copy(0, 0, r).start()
            return c
        lax.fori_loop(0, EXPERT_BLOCK, body, 0)

    prev = be_ref[jnp.maximum(i - 1, 0)]

    @pl.when((i == 0) | (be_ref[i] != prev))
    def _():
        wgu_b[...] = wgu_ref[0, 0].astype(BF16)
        wdn_b[...] = wdn_ref[0, 0].astype(BF16)

    nxt = jnp.minimum(i + 1, nblk - 1)
    for par in range(2):
        @pl.when(slot == par)
        def _(par=par):
            wait_rows(i, par)
            for r in range(EXPERT_BLOCK):
                row_copy(nxt, 1 - par, r).start()
            gu = jnp.dot(bufs[par][...].astype(BF16), wgu_b[...], preferred_element_type=F32) + bgu_ref[0, 0]
            g = jnp.minimum(gu[:, :dff], SWIGLU_LIMIT)
            u = jnp.clip(gu[:, dff:], -SWIGLU_LIMIT, SWIGLU_LIMIT)
            act = g * jax.nn.sigmoid(SWIGLU_ALPHA * g) * (u + 1.0)
            y_ref[...] = jnp.dot(act.astype(BF16), wdn_b[...], preferred_element_type=F32) + bdn_ref[0, 0]

            @pl.when(i == nblk - 1)
            def _():
                wait_rows(nxt, 1 - par)


def _moe_blocks(l, h2, blk_expert, src, n_blocks, exp_w_gu, exp_b_gu, exp_w_dn, exp_b_dn):
    rtot, d = h2.shape
    dff = exp_w_dn.shape[2]
    bgu = exp_b_gu.reshape(exp_b_gu.shape[0], N_EXPERTS, 1, 2 * dff)
    bdn = exp_b_dn.reshape(exp_b_dn.shape[0], N_EXPERTS, 1, d)
    gs = pltpu.PrefetchScalarGridSpec(
        num_scalar_prefetch=2,
        grid=(n_blocks,),
        in_specs=[pl.BlockSpec(memory_space=pl.ANY),
                  pl.BlockSpec((1, 1, d, 2 * dff), lambda i, be, s: (l, be[i], 0, 0)),
                  pl.BlockSpec((1, 1, 1, 2 * dff), lambda i, be, s: (l, be[i], 0, 0)),
                  pl.BlockSpec((1, 1, dff, d), lambda i, be, s: (l, be[i], 0, 0)),
                  pl.BlockSpec((1, 1, 1, d), lambda i, be, s: (l, be[i], 0, 0))],
        out_specs=pl.BlockSpec((EXPERT_BLOCK, d), lambda i, be, s: (i, 0)),
        scratch_shapes=[pltpu.VMEM((EXPERT_BLOCK, d), F32), pltpu.VMEM((EXPERT_BLOCK, d), F32),
                        pltpu.SemaphoreType.DMA((2,)),
                        pltpu.VMEM((d, 2 * dff), BF16), pltpu.VMEM((dff, d), BF16)])
    return pl.pallas_call(
        functools.partial(_moe_kernel, nblk=n_blocks, dff=dff),
        grid_spec=gs,
        out_shape=jax.ShapeDtypeStruct((n_blocks * EXPERT_BLOCK, d), F32),
        compiler_params=_cparams(("arbitrary",)),
        name="moe_experts",
    )(blk_expert, src, h2, exp_w_gu, bgu, exp_w_dn, bdn)


def _final_kernel(dest_ref, x1_ref, y_hbm, tg_ref, g2_ref, lng_ref, lnb_ref, o_ref, ybuf, sem,
                  *, nsteps, tr, alpha):
    i = pl.program_id(0)
    slot = i % 2

    def row_copy(step, sl, r, k):
        pos = dest_ref[(step * tr + r) * TOP_K + k]
        return pltpu.make_async_copy(y_hbm.at[pl.ds(pos, 1)], ybuf.at[sl, k, pl.ds(r, 1)], sem.at[sl])

    def issue(step, sl):
        def body(r, c):
            for k in range(TOP_K):
                row_copy(step, sl, r, k).start()
            return c
        lax.fori_loop(0, tr, body, 0)

    @pl.when(i == 0)
    def _():
        issue(0, 0)

    @pl.when(i + 1 < nsteps)
    def _():
        issue(i + 1, 1 - slot)

    for k in range(TOP_K):
        pltpu.make_async_copy(y_hbm.at[pl.ds(0, tr)], ybuf.at[slot, k], sem.at[slot]).wait()

    tg = tg_ref[...]
    y = tg[:, 0:1] * ybuf[slot, 0]
    for k in range(1, TOP_K):
        y = y + tg[:, k:k + 1] * ybuf[slot, k]
    o_ref[...] = _layer_norm(alpha * x1_ref[...] + g2_ref[0] * y, lng_ref[...], lnb_ref[...])


def _combine_final(x1, y_pad, dest, gates, gate2, ln_g, ln_b, rows_per_mod, alpha):
    rtot, d = x1.shape
    tr = min(FINAL_TILE, rtot)
    nsteps = rtot // tr
    gate2 = _tile_mod(gate2, tr)
    mod_map = _mod_map(gate2.shape[1], tr, rows_per_mod)
    gs = pltpu.PrefetchScalarGridSpec(
        num_scalar_prefetch=1,
        grid=(nsteps,),
        in_specs=[pl.BlockSpec((tr, d), lambda r, dst: (r, 0)),
                  pl.BlockSpec(memory_space=pl.ANY),
                  pl.BlockSpec((tr, LANES), lambda r, dst: (r, 0)),
                  pl.BlockSpec((1, gate2.shape[1], d), mod_map),
                  pl.BlockSpec(ln_g.shape, lambda r, dst: (0, 0)),
                  pl.BlockSpec(ln_b.shape, lambda r, dst: (0, 0))],
        out_specs=pl.BlockSpec((tr, d), lambda r, dst: (r, 0)),
        scratch_shapes=[pltpu.VMEM((2, TOP_K, tr, d), F32), pltpu.SemaphoreType.DMA((2,))])
    return pl.pallas_call(
        functools.partial(_final_kernel, nsteps=nsteps, tr=tr, alpha=alpha),
        grid_spec=gs,
        out_shape=jax.ShapeDtypeStruct((rtot, d), F32),
        compiler_params=_cparams(("arbitrary",)),
        name="moe_combine_norm",
    )(dest, x1, y_pad, gates, gate2, ln_g, ln_b)


def _chunk_copies(specs, sem, layer, pt_ref, b, chunk, slot, pages):
    cps = []
    for ci, (cache, buf, placement) in enumerate(specs):
        for p in range(pages):
            page = pt_ref[b, chunk * pages + p]
            src = cache.at[layer, page]
            if placement == 'rows':
                dst = buf.at[slot, pl.ds(p * LANES, LANES)]
            elif placement == 'page':
                dst = buf.at[slot, :, p, :]
            else:
                idx = (slot,) + (slice(None),) * (len(buf.shape) - 2) + (pl.ds(p * LANES, LANES),)
                dst = buf.at[idx]
            cps.append(pltpu.make_async_copy(src, dst, sem.at[slot, ci]))
    return cps


def _pipeline_fetch(specs, sem, layer, pt_ref, nch, pages, chunk_of):
    b, c = pl.program_id(0), pl.program_id(1)
    nb = pl.num_programs(0)
    g = b * nch + c
    slot = g % 2

    @pl.when(g == 0)
    def _():
        for cp in _chunk_copies(specs, sem, layer, pt_ref, b, chunk_of(c), slot, pages):
            cp.start()

    @pl.when(g + 1 < nb * nch)
    def _():
        g1 = g + 1
        b1, c1 = g1 // nch, g1 % nch
        for cp in _chunk_copies(specs, sem, layer, pt_ref, b1, chunk_of(c1), 1 - slot, pages):
            cp.start()

    for cp in _chunk_copies(specs, sem, layer, pt_ref, b, chunk_of(c), slot, pages):
        cp.wait()
    return slot


def _stack_slabs(x, n, width=1):
    return jnp.concatenate([x[:, h * width * LANES:(h + 1) * width * LANES] for h in range(n)], axis=0)


def _dec_idx_kernel(pt_ref, kidx_hbm, logf_hbm, qi_ref, iw_ref, knew_ref, kp_ref, kn_ref, thr_ref, fb_ref,
                    kbuf, lbuf, sem, allk, newk, *, layer, nch, pages, dec, nsel):
    c = pl.program_id(1)
    slot = _pipeline_fetch([(kidx_hbm, kbuf, 'lanes'), (logf_hbm, lbuf, 'page')], sem, layer, pt_ref, nch, pages,
                           lambda cc: cc)
    pj = lax.broadcasted_iota(I32, (LANES, LANES), 0)
    ps = lax.broadcasted_iota(I32, (LANES, LANES), 1)
    within = jnp.where(pj >= ps, 1.0, 0.0)
    gi = lax.broadcasted_iota(I32, (pages, pages), 0)
    gj = lax.broadcasted_iota(I32, (pages, pages), 1)
    later = jnp.where(gj > gi, 1.0, 0.0)
    hp = lax.Precision.HIGHEST
    for h in range(N_HEADS):
        x = lbuf[slot, h]
        incl = jnp.dot(x, within, precision=hp, preferred_element_type=F32)
        tot = jnp.broadcast_to(incl[:, 0:1], (pages, LANES))
        fb_ref[h] = incl - x + jnp.dot(later, tot, precision=hp, preferred_element_type=F32)

    q = _stack_slabs(qi_ref[...], IDX_HEADS)[:, :IDX_DIM]
    iw = iw_ref[...]
    wcol = jnp.concatenate([iw[:, h:h + 1] for h in range(IDX_HEADS)], axis=0)
    wcol = wcol * (IDX_HEADS ** -0.5 * IDX_DIM ** -0.5)

    def scores(kt):
        r = jnp.maximum(_bdot(q, kt), 0.0) * wcol
        sc = r[0:dec]
        for h in range(1, IDX_HEADS):
            sc = sc + r[h * dec:(h + 1) * dec]
        return _score_keys(sc)

    key = scores(kbuf[slot])
    kp_ref[...] = key
    allk[c] = key

    @pl.when(c == nch - 1)
    def _():
        kn = scores(knew_ref[...])
        t = lax.broadcasted_iota(I32, kn.shape, 0)
        s = lax.broadcasted_iota(I32, kn.shape, 1)
        kn = jnp.where(s <= t, kn, INT_MIN)
        kn_ref[...] = kn
        newk[...] = kn

        def count_ge(cand):
            def cbody(j, acc):
                ge = jnp.where(allk[j] >= cand, 1.0, 0.0)
                parts = [ge[:, i * LANES:(i + 1) * LANES] for i in range(ge.shape[1] // LANES)]
                while len(parts) > 1:
                    parts = [a + b for a, b in zip(parts[0::2], parts[1::2])] + (parts[-1:] if len(parts) % 2 else [])
                return acc + parts[0]
            acc = lax.fori_loop(0, nch, cbody, jnp.where(newk[...] >= cand, 1.0, 0.0))
            return jnp.sum(acc, axis=1, keepdims=True)

        zero = jnp.zeros((dec, 1), I32)
        t0 = jnp.where(count_ge(zero) >= nsel, zero, INT_MIN)

        def bit_body(bi, tcur):
            cand = tcur + jnp.left_shift(jnp.int32(1), 30 - bi)
            return jnp.where(count_ge(cand) >= nsel, cand, tcur)

        thr = lax.fori_loop(0, 31, bit_body, t0)
        thr_ref[...] = jnp.broadcast_to(thr, thr_ref.shape)


def _suffix_sum(x):
    n = x.shape[1]
    lane = lax.broadcasted_iota(I32, x.shape, 1)
    d = 1
    while d < n:
        x = x + jnp.where(lane + d < n, pltpu.roll(x, n - d, 1), 0.0)
        d *= 2
    return x


def _online_update_fn(s, pv, m_ref, l_ref, acc_ref, keep=None):
    m_prev = m_ref[...]
    m_new = jnp.maximum(m_prev, jnp.max(s, axis=1, keepdims=True))
    alpha = jnp.exp(m_prev - m_new)
    p = jnp.exp(s - m_new)
    if keep is not None:
        p = jnp.where(keep, p, 0.0)
    l_ref[...] = alpha * l_ref[...] + jnp.sum(p, axis=1, keepdims=True)
    acc_ref[...] = alpha * acc_ref[...] + pv(p)
    m_ref[...] = m_new


def _dec_att_kernel(pt_ref, lam_ref,
                    fox_hbm, ckv_hbm, kr_hbm, diff_hbm, dsa_hbm,
                    qf_ref, qm_ref, qd_ref, qs_ref, ecol_ref, fb_ref, kp_ref, kn_ref, thr_ref,
                    nf_ref, nl_ref, nc_ref, nk_ref, nd_ref, ns_ref,
                    tbd_ref, tbs_ref, tnd_ref, tns_ref, g_ref,
                    of_ref, om_ref, od_ref, os_ref,
                    fbuf, cbuf, kbuf, dbuf, sbuf, sem,
                    mf, lf_, af, mm, lm, am, md, ld, ad, ms, ls, as_, rcar,
                    *, layer, nch, pages, dec, lam_init):
    c = pl.program_id(1)
    specs = [(fox_hbm, fbuf, 'lanes'), (ckv_hbm, cbuf, 'rows'), (kr_hbm, kbuf, 'lanes'),
             (diff_hbm, dbuf, 'lanes'), (dsa_hbm, sbuf, 'lanes')]
    slot = _pipeline_fetch(specs, sem, layer, pt_ref, nch, pages, lambda cc: nch - 1 - cc)
    nr = N_HEADS * dec

    qf = _stack_slabs(qf_ref[...], N_HEADS)[:, :HEAD_DIM]
    qm = _stack_slabs(qm_ref[...], N_HEADS, 2)
    qlat, qrope = qm[:, :LANES], qm[:, LANES:LANES + MLA_ROPE]
    qd_all = qd_ref[...]
    qd = jnp.concatenate([qd_all[:, (2 * h + g) * LANES:(2 * h + g + 1) * LANES]
                          for g in range(2) for h in range(N_HEADS)], axis=0)[:, :HEAD_DIM]
    qs = _stack_slabs(qs_ref[...], N_HEADS)[:, :HEAD_DIM]
    ecol = ecol_ref[...]
    thr = thr_ref[:, 0:1]

    def process(fkv, after, ckv, krt, dkv, skv, keys, t5d, t5s, is_new):
        nk = after.shape[1]
        if is_new:
            trow = lax.broadcasted_iota(I32, (dec, nk), 0)
            scol = lax.broadcasted_iota(I32, (dec, nk), 1)
            ok8 = jnp.where(scol <= trow, 1.0, 0.0)
            ok = jnp.concatenate([ok8] * N_HEADS, axis=0) > 0.5
            ok2 = jnp.concatenate([ok8] * (2 * N_HEADS), axis=0) > 0.5
        s = _bdot(qf, fkv[0]) * HEAD_DIM ** -0.5
        excl = after + rcar[...]
        bias = jnp.concatenate([jnp.broadcast_to(excl[h:h + 1, :], (dec, nk)) for h in range(N_HEADS)], axis=0)
        s = s + bias - ecol
        if is_new:
            s = jnp.where(ok, s, NEG_INF)
        _online_update_fn(s, lambda p: _bdot_nt(p, fkv[1]), mf, lf_, af)
        s = (_bdot_nt(qlat, ckv) + _bdot(qrope, krt)) * (MLA_NOPE + MLA_ROPE) ** -0.5
        if is_new:
            s = jnp.where(ok, s, NEG_INF)
        _online_update_fn(s, lambda p: _bdot(p, ckv), mm, lm, am)
        s = _bdot(qd, dkv[0]) * DIFF_HALF ** -0.5 + t5d
        if is_new:
            s = jnp.where(ok2, s, NEG_INF)
        _online_update_fn(s, lambda p: _bdot_nt(p, dkv[1]), md, ld, ad)
        s = _bdot(qs, skv[0]) * HEAD_DIM ** -0.5 + t5s
        sel8 = jnp.where(keys >= thr, 1.0, 0.0)
        if is_new:
            sel8 = sel8 * ok8
        keep = jnp.concatenate([sel8] * N_HEADS, axis=0) > 0.5
        s = jnp.where(keep, s, NEG_INF)
        _online_update_fn(s, lambda p: _bdot_nt(p, skv[1]), ms, ls, as_, keep=keep)

    @pl.when(c == 0)
    def _():
        for m_, l_, a_ in ((mf, lf_, af), (mm, lm, am), (md, ld, ad), (ms, ls, as_)):
            _init_state(m_, l_, a_)
        rcar[...] = jnp.zeros_like(rcar)
        lgf_new = nl_ref[...]
        incl_new = _suffix_sum(lgf_new)
        process(nf_ref[...], incl_new - lgf_new, nc_ref[...], nk_ref[...], nd_ref[...], ns_ref[...], kn_ref[...],
                tnd_ref[...], tns_ref[...], True)
        rcar[...] = incl_new[:, 0:1]

    process(fbuf[slot], fb_ref[...], cbuf[slot], kbuf[slot], dbuf[slot], sbuf[slot], kp_ref[...],
            tbd_ref[...], tbs_ref[...], False)

    @pl.when(c == nch - 1)
    def _():
        def put(o_ref, o):
            w = o.shape[1]
            for h in range(N_HEADS):
                if w < LANES:
                    o_ref[h] = jnp.zeros((dec, LANES), BF16)
                    o_ref[h, :, 0:w] = o[h * dec:(h + 1) * dec].astype(BF16)
                else:
                    o_ref[h] = o[h * dec:(h + 1) * dec].astype(BF16)

        put(of_ref, af[...] / lf_[...])
        put(om_ref, am[...] / lm[...])
        od = ad[...] / ld[...]
        od = od[:nr] - lam_ref[0] * od[nr:]
        y = od * lax.rsqrt(jnp.mean(od * od, axis=-1, keepdims=True) + EPS) * g_ref[...] * (1.0 - lam_init)
        put(od_ref, y)
        put(os_ref, as_[...] / ls[...])


def _split_rows(rows, lead):
    s = lambda i, n: rows[:, i * LANES:i * LANES + n]
    return {
        'fox_kv': s(0, 2 * HEAD_DIM).reshape(lead + (2, HEAD_DIM)),
        'fox_logf': s(1, N_HEADS).reshape(lead + (N_HEADS,)),
        'mla_ckv': s(2, MLA_KV_RANK).reshape(lead + (MLA_KV_RANK,)),
        'mla_krope': s(3, MLA_ROPE).reshape(lead + (MLA_ROPE,)),
        'diff_kv': s(4, 2 * HEAD_DIM).reshape(lead + (2, HEAD_DIM)),
        'dsa_kv': s(5, 2 * HEAD_DIM).reshape(lead + (2, HEAD_DIM)),
        'dsa_kidx': s(6, IDX_DIM).reshape(lead + (IDX_DIM,)),
    }


def _ffn_and_norm(l, routed, gate2, ln_g2, ln_b2, ew, rows_per_mod, alpha):
    x1, h2, ti, tg, rk, cnt = routed
    n_tok = x1.shape[0]
    counts = cnt[0, :N_EXPERTS].astype(I32)
    dest, src, blk_expert, n_blocks = _routing_tables(ti[:, :TOP_K], rk[:, :TOP_K], counts, n_tok)
    y_pad = _moe_blocks(l, h2, blk_expert, src, n_blocks, *ew)
    return _combine_final(x1, y_pad, dest, tg, gate2, ln_g2, ln_b2, rows_per_mod, alpha)


def _lambda(l, diff_lambda):
    lam_init = 0.8 - 0.6 * math.exp(-0.3 * l)
    dl = diff_lambda[l]
    lam = (jnp.exp(jnp.sum(dl[0] * dl[1]).astype(F32)) - jnp.exp(jnp.sum(dl[2] * dl[3]).astype(F32)) + lam_init)
    return lam, lam_init


def _decode_attention(l, pr, caches_t, page_table, t5_table, lam, lam_init, subln_lo, bd, dec):
    rows, cum, qf, qm, qd, qs, qi, iw = pr
    fox_t, logf_t, ckv_c, kr_t, diff_t, dsa_t, kidx_t = caches_t
    n_pages = page_table.shape[1]
    pages = min(PAGES_PER_STEP, n_pages)
    assert n_pages % pages == 0
    nch = n_pages // pages
    gk = pages * LANES
    nk_past = n_pages * LANES
    nsel = min(IDX_TOPK, (nk_past + dec) // 4)

    rows3 = rows.reshape(bd, dec, rows.shape[-1])

    def key_minor(slab, n):
        x = jnp.transpose(rows3[:, :, slab * LANES:slab * LANES + n], (0, 2, 1))
        return _pad_last(x, LANES)

    nf = key_minor(0, 2 * HEAD_DIM).reshape(bd, 2, HEAD_DIM, LANES)
    nl = key_minor(1, N_HEADS)
    nc = jnp.pad(rows3[:, :, 2 * LANES:3 * LANES], ((0, 0), (0, LANES - dec), (0, 0)))
    nkr = key_minor(3, MLA_ROPE)
    nd = key_minor(4, 2 * HEAD_DIM).reshape(bd, 2, HEAD_DIM, LANES)
    ns = key_minor(5, 2 * HEAD_DIM).reshape(bd, 2, HEAD_DIM, LANES)
    nidx = key_minor(6, IDX_DIM)
    cum3 = cum.reshape(bd, dec, LANES)[:, :, :N_HEADS]
    ecol = jnp.transpose(cum3[:, dec - 1:dec, :] - cum3, (0, 2, 1)).reshape(bd, N_HEADS * dec, 1)

    sem = ("arbitrary", "arbitrary")
    row_blk = lambda n: pl.BlockSpec((dec, n), lambda b, c, *_: (b, 0))
    per_b = lambda a: pl.BlockSpec((None,) + a.shape[1:], lambda b, c, *_: (b,) + (0,) * (a.ndim - 1))
    anyspec = pl.BlockSpec(memory_space=pl.ANY)

    gs_idx = pltpu.PrefetchScalarGridSpec(
        num_scalar_prefetch=1, grid=(bd, 1),
        in_specs=[anyspec, anyspec, row_blk(8 * LANES), row_blk(LANES), per_b(nidx)],
        out_specs=[pl.BlockSpec((None, dec, nk_past), lambda b, c, pt: (b, 0, 0)),
                   pl.BlockSpec((None, dec, LANES), lambda b, c, pt: (b, 0, 0)),
                   pl.BlockSpec((None, dec, LANES), lambda b, c, pt: (b, 0, 0)),
                   pl.BlockSpec((None, N_HEADS, n_pages, LANES), lambda b, c, pt: (b, 0, 0, 0))],
        scratch_shapes=[pltpu.VMEM((2, IDX_DIM, nk_past), F32), pltpu.VMEM((2, N_HEADS, n_pages, LANES), F32),
                        pltpu.SemaphoreType.DMA((2, 2)),
                        pltpu.VMEM((1, dec, nk_past), I32), pltpu.VMEM((dec, LANES), I32)])
    keys_past, keys_new, thr, after = pl.pallas_call(
        functools.partial(_dec_idx_kernel, layer=l, nch=1, pages=n_pages, dec=dec, nsel=nsel),
        grid_spec=gs_idx,
        out_shape=[jax.ShapeDtypeStruct((bd, dec, nk_past), I32), jax.ShapeDtypeStruct((bd, dec, LANES), I32),
                   jax.ShapeDtypeStruct((bd, dec, LANES), I32),
                   jax.ShapeDtypeStruct((bd, N_HEADS, n_pages, LANES), F32)],
        compiler_params=_cparams(sem), name="decode_indexer",
    )(page_table, kidx_t, logf_t, qi, iw, nidx)
    after = after.reshape(bd, N_HEADS, nk_past)

    tpos = jnp.arange(dec, dtype=I32)[:, None]
    rel_last = gk + tpos - jnp.arange(gk, dtype=I32)[None, :]
    rel_far = jnp.full((dec, gk), 2 * MAX_DISTANCE, I32)
    rel_new = tpos - jnp.arange(LANES, dtype=I32)[None, :]

    def tiles(cols, reps):
        tb = jnp.stack([_t5_rows(cols, rel_far, reps), _t5_rows(cols, rel_last, reps)], axis=0).astype(F32)
        return tb, _t5_rows(cols, rel_new, reps).astype(F32)

    tbd, tnd = tiles(t5_table[:, :N_HEADS], 2)
    tbs, tns = tiles(t5_table[:, N_HEADS:], 1)
    recent = lambda a: pl.BlockSpec((None,) + a.shape[1:], lambda b, c, *_: (jnp.where(c == 0, 1, 0), 0, 0))
    full = lambda a: pl.BlockSpec(a.shape, lambda b, c, *_: (0,) * a.ndim)
    nr = N_HEADS * dec
    col = lambda n: pltpu.VMEM((n, 1), F32)
    ospec = pl.BlockSpec((None, N_HEADS, dec, LANES), lambda b, c, *_: (b, 0, 0, 0))
    oshape = jax.ShapeDtypeStruct((bd, N_HEADS, dec, LANES), BF16)
    gs_att = pltpu.PrefetchScalarGridSpec(
        num_scalar_prefetch=1, grid=(bd, nch),
        in_specs=[pl.BlockSpec(memory_space=pltpu.SMEM)] + [anyspec] * 5 + [
            row_blk(4 * LANES), row_blk(8 * LANES), row_blk(8 * LANES), row_blk(4 * LANES), per_b(ecol),
            pl.BlockSpec((None, N_HEADS, gk), lambda b, c, pt: (b, 0, nch - 1 - c)),
            pl.BlockSpec((None, dec, gk), lambda b, c, pt: (b, 0, nch - 1 - c)), per_b(keys_new), per_b(thr),
            per_b(nf), per_b(nl), per_b(nc), per_b(nkr), per_b(nd), per_b(ns),
            recent(tbd), recent(tbs), full(tnd), full(tns), full(subln_lo)],
        out_specs=[ospec] * 4,
        scratch_shapes=[pltpu.VMEM((2, 2, HEAD_DIM, gk), F32),
                        pltpu.VMEM((2, gk, LANES), F32), pltpu.VMEM((2, MLA_ROPE, gk), F32),
                        pltpu.VMEM((2, 2, HEAD_DIM, gk), F32), pltpu.VMEM((2, 2, HEAD_DIM, gk), F32),
                        pltpu.SemaphoreType.DMA((2, 5)),
                        col(nr), col(nr), pltpu.VMEM((nr, HEAD_DIM), F32),
                        col(nr), col(nr), pltpu.VMEM((nr, LANES), F32),
                        col(2 * nr), col(2 * nr), pltpu.VMEM((2 * nr, HEAD_DIM), F32),
                        col(nr), col(nr), pltpu.VMEM((nr, HEAD_DIM), F32),
                        pltpu.VMEM((N_HEADS, 1), F32)])
    return pl.pallas_call(
        functools.partial(_dec_att_kernel, layer=l, nch=nch, pages=pages, dec=dec, lam_init=lam_init),
        grid_spec=gs_att,
        out_shape=[oshape] * 4,
        compiler_params=_cparams(sem), name="decode_attention",
    )(page_table, lam.reshape(1), fox_t, ckv_c, kr_t, diff_t, dsa_t,
      qf, qm, qd, qs, ecol, after, keys_past, keys_new, thr, nf, nl, nc, nkr, nd, ns, tbd, tbs, tnd, tns, subln_lo)


def _sample_layer(l, x, mod, wts, w_out_l, ln_g, ln_b, ew, t5_table, diff_lambda, diff_subln, alpha,
                  caches_t, page_table):
    bd, dec, d = x.shape
    past_len = page_table.shape[1] * LANES
    mod_rows = jnp.repeat(mod, dec, axis=0)
    m = [mod_rows[:, k * d:(k + 1) * d] for k in range(6)]
    cos_t, sin_t = _rope_tables(past_len + jnp.arange(dec, dtype=I32))
    cos_t, sin_t = jnp.tile(cos_t, (bd, 1)), jnp.tile(sin_t, (bd, 1))
    x2d = x.reshape(bd * dec, d)
    pr = _project(x2d, m[1], m[0], wts, cos_t, sin_t, dec, 1, False)
    lam, lam_init = _lambda(l, diff_lambda)
    o_list = _decode_attention(l, pr, caches_t, page_table, t5_table, lam, lam_init, diff_subln[l][None, :], bd, dec)
    wo = _prep_w_out(w_out_l, False)
    routed = _out_router(x2d, o_list, wo, wts['wv'], m[2], m[4], m[3], ln_g[l, 0:1], ln_b[l, 0:1],
                         wts['rw'], wts['rb'], 1, alpha)
    x2 = _ffn_and_norm(l, routed, m[5], ln_g[l, 1:2], ln_b[l, 1:2], ew, 1, alpha)
    return x2.reshape(bd, dec, d), _split_rows(pr[0], (bd, dec))


def _prompt_layer(l, x, mod, wts, w_out_l, ln_g, ln_b, ew, t5_table, diff_lambda, alpha):
    bsz, seq, d = x.shape
    m = [mod[:, None, k * d:(k + 1) * d] for k in range(6)]
    cos_t, sin_t = _rope_tables(jnp.arange(seq, dtype=I32))
    cos_t, sin_t = jnp.tile(cos_t, (bsz, 1)), jnp.tile(sin_t, (bsz, 1))
    x2d = x.reshape(bsz * seq, d)
    pr = _project(x2d, m[1], m[0], wts, cos_t, sin_t, seq, seq, True)
    lam, lam_init = _lambda(l, diff_lambda)
    o_list = _prompt_attention(pr, t5_table, lam, lam_init, wts['subln'], bsz, seq)
    wo = _prep_w_out(w_out_l, True)
    routed = _out_router(x2d, o_list, wo, wts['wv'], m[2], m[4], m[3], ln_g[l, 0:1], ln_b[l, 0:1],
                         wts['rw'], wts['rb'], seq, alpha)
    x2 = _ffn_and_norm(l, routed, m[5], ln_g[l, 1:2], ln_b[l, 1:2], ew, seq, alpha)
    return x2.reshape(bsz, seq, d), _split_rows(pr[0], (bsz, seq))


def kernel(x_prompt, x_sample, c_prompt, c_sample, cache_fox_kv, cache_fox_logf, cache_mla_ckv, cache_mla_krope,
           cache_diff_kv, cache_dsa_kv, cache_dsa_kidx, page_table, w_in, fox_f_bias, mla_q_norm, mla_w_uq,
           mla_kv_norm, mla_w_uk, mla_w_uv, diff_lambda, diff_subln, w_out, w_ada, b_ada, ln_g, ln_b,
           router_w, router_b, exp_w_gu, exp_b_gu, exp_w_dn, exp_b_dn, t5_table):
    depth = w_in.shape[0]
    alpha = (2 * depth) ** 0.25
    bsz, bd = x_prompt.shape[0], x_sample.shape[0]
    caches_t = (jnp.transpose(cache_fox_kv, (0, 1, 3, 4, 2)), jnp.transpose(cache_fox_logf, (0, 1, 3, 2)),
                cache_mla_ckv, jnp.transpose(cache_mla_krope, (0, 1, 3, 2)),
                jnp.transpose(cache_diff_kv, (0, 1, 3, 4, 2)), jnp.transpose(cache_dsa_kv, (0, 1, 3, 4, 2)),
                jnp.transpose(cache_dsa_kidx, (0, 1, 3, 2)))
    n_c = bsz + bd
    c_all = jnp.pad(jnp.concatenate([c_prompt, c_sample], axis=0), ((0, -n_c % 8), (0, 0)))
    mod_all = _ada(c_all, w_ada, b_ada)
    ew = (exp_w_gu, exp_b_gu, exp_w_dn, exp_b_dn)
    xp, xs = x_prompt, x_sample
    rows_p, rows_s = [], []
    for l in range(depth):
        wts = _prep_layer_weights(l, w_in, fox_f_bias, mla_q_norm, mla_w_uq, mla_kv_norm, mla_w_uk, mla_w_uv,
                                  diff_subln, w_out, router_w, router_b)
        xp, rp = _prompt_layer(l, xp, mod_all[l, :bsz], wts, w_out[l], ln_g, ln_b, ew, t5_table, diff_lambda, alpha)
        xs, rs = _sample_layer(l, xs, mod_all[l, bsz:n_c], wts, w_out[l], ln_g, ln_b, ew, t5_table, diff_lambda,
                               diff_subln, alpha, caches_t, page_table)
        rows_p.append(rp)
        rows_s.append(rs)

    def stack(rows, name):
        return jnp.stack([r[name] for r in rows], axis=0)

    out = [xp, xs]
    for name in ('fox_kv', 'fox_logf', 'mla_ckv', 'mla_krope', 'diff_kv', 'dsa_kv', 'dsa_kidx'):
        out += [stack(rows_p, name), stack(rows_s, name)]
    return tuple(out)
```

```python
import functools
import math

import numpy as np
import jax
import jax.numpy as jnp
from jax import lax
from jax.experimental import pallas as pl
from jax.experimental.pallas import tpu as pltpu

F32 = jnp.float32
BF16 = jnp.bfloat16
I32 = jnp.int32

LANES = 128
HEAD_DIM = 64
N_HEADS = 4
MLA_NOPE, MLA_ROPE, MLA_V, MLA_KV_RANK, MLA_Q_RANK = 64, 32, 64, 128, 192
DIFF_HALF = HEAD_DIM // 2
IDX_HEADS, IDX_DIM, IDX_TOPK = 8, 32, 256
ROPE_BASE = 10000.0
N_BUCKETS, MAX_DISTANCE = 32, 128
N_EXPERTS, TOP_K = 32, 4
EXPERT_BLOCK = 256
SWIGLU_LIMIT, SWIGLU_ALPHA = 7.0, 1.702
NEG_INF = -1e30
EPS = 1e-5
INT_MIN = -(2 ** 31)

IN_SIZES = (256, 64, 64, 4, 192, 128, 32, 256, 64, 64, 256, 64, 64, 256, 32, 8)
IN_OFFS = tuple(int(v) for v in np.cumsum((0,) + IN_SIZES))

S_FQ, S_FKV, S_FF, S_CQ, S_CKV, S_KR, S_DQ, S_DKV, S_SQ, S_SKV, S_IQ, S_IK, S_IW = (
    0, 4, 5, 6, 8, 9, 10, 18, 19, 23, 24, 32, 33)
N_SLABS = 34

ROW_TILE = 256
Q_TILE = 256
FINAL_TILE = 128
PAGES_PER_STEP = 32
VMEM_LIMIT = 56 * 1024 * 1024


def _cparams(sem, vmem=VMEM_LIMIT):
    return pltpu.CompilerParams(dimension_semantics=sem, vmem_limit_bytes=vmem)


def _bdot(a, b):
    return jnp.dot(a.astype(BF16), b.astype(BF16), preferred_element_type=F32)


def _bdot_nt(a, b):
    return lax.dot_general(a.astype(BF16), b.astype(BF16), (((1,), (1,)), ((), ())),
                           preferred_element_type=F32)


def _layer_norm(z, g, b):
    mu = jnp.mean(z, axis=-1, keepdims=True)
    var = jnp.mean(jnp.square(z - mu), axis=-1, keepdims=True)
    return (z - mu) * lax.rsqrt(var + EPS) * g + b


def _in_proj_columns():
    idx = np.full((N_SLABS * LANES,), -1, np.int64)

    def put(slab, lane0, src0, n):
        idx[slab * LANES + lane0: slab * LANES + lane0 + n] = np.arange(src0, src0 + n)

    o = IN_OFFS
    for h in range(N_HEADS):
        put(S_FQ + h, 0, o[0] + h * HEAD_DIM, HEAD_DIM)
        put(S_DQ + 2 * h, 0, o[7] + h * HEAD_DIM, DIFF_HALF)
        put(S_DQ + 2 * h + 1, DIFF_HALF, o[7] + h * HEAD_DIM + DIFF_HALF, DIFF_HALF)
        put(S_SQ + h, 0, o[10] + h * HEAD_DIM, HEAD_DIM)
    put(S_FKV, 0, o[1], 2 * HEAD_DIM)
    put(S_FF, 0, o[3], N_HEADS)
    put(S_CQ, 0, o[4], MLA_Q_RANK)
    put(S_CKV, 0, o[5], MLA_KV_RANK)
    put(S_KR, 0, o[6], MLA_ROPE)
    put(S_DKV, 0, o[8], 2 * HEAD_DIM)
    put(S_SKV, 0, o[11], 2 * HEAD_DIM)
    for h in range(IDX_HEADS):
        put(S_IQ + h, 0, o[13] + h * IDX_DIM, IDX_DIM)
    put(S_IK, 0, o[14], IDX_DIM)
    put(S_IW, 0, o[15], IDX_HEADS)
    return idx


def _column_runs(idx):
    runs, i = [], 0
    while i < len(idx):
        j = i + 1
        if idx[i] < 0:
            while j < len(idx) and idx[j] < 0:
                j += 1
            runs.append((0, j - i, True))
        else:
            while j < len(idx) and idx[j] == idx[j - 1] + 1:
                j += 1
            runs.append((int(idx[i]), j - i, False))
        i = j
    return tuple(runs)


_IN_RUNS = _column_runs(_in_proj_columns())


def _pad_last(a, n):
    return jnp.pad(a, [(0, 0)] * (a.ndim - 1) + [(0, n - a.shape[-1])])


def _prep_layer_weights(l, w_in, fox_f_bias, mla_q_norm, mla_w_uq, mla_kv_norm, mla_w_uk, mla_w_uv,
                        diff_subln, w_out, router_w, router_b):
    pieces = []
    for start, n, is_zero in _IN_RUNS:
        if is_zero:
            pieces.append(jnp.zeros((w_in.shape[1], n), w_in.dtype))
        else:
            pieces.append(w_in[l, :, start:start + n])
    w_in_p = jnp.concatenate(pieces, axis=1).astype(BF16)
    fb = _pad_last(fox_f_bias[l][None, :], LANES)
    gq = _pad_last(mla_q_norm[l][None, :], 2 * LANES)
    gkv = mla_kv_norm[l][None, :]
    wuq = mla_w_uq[l].reshape(MLA_Q_RANK, N_HEADS, MLA_NOPE + MLA_ROPE)
    nope = _pad_last(wuq[:, :, :MLA_NOPE], LANES).reshape(MLA_Q_RANK, N_HEADS * LANES)
    ropew = _pad_last(wuq[:, :, MLA_NOPE:], LANES).reshape(MLA_Q_RANK, N_HEADS * LANES)
    wuq_p = jnp.pad(jnp.concatenate([nope, ropew], axis=1), ((0, 2 * LANES - MLA_Q_RANK), (0, 0))).astype(BF16)
    wk = jnp.transpose(mla_w_uk[l], (1, 2, 0))
    wk = jnp.pad(wk, ((0, 0), (0, LANES - MLA_NOPE), (0, 0))).astype(BF16)
    wv = jnp.transpose(mla_w_uv[l], (1, 0, 2))
    wv = _pad_last(wv, LANES).astype(BF16)
    subln = jnp.pad(diff_subln[l][None, :], ((0, 0), (HEAD_DIM, 0)))
    rw = _pad_last(router_w[l], LANES)
    rb = jnp.pad(router_b[l][None, :], ((0, 0), (0, LANES - N_EXPERTS)), constant_values=-3e38)
    return dict(w_in_p=w_in_p, fb=fb, gq=gq, gkv=gkv, wuq_p=wuq_p, wk=wk, wv=wv, subln=subln, rw=rw, rb=rb)


def _prep_w_out(w_out_l, value_in_upper_half):
    w = w_out_l.reshape(4, N_HEADS, HEAD_DIM, -1)
    lo = jnp.pad(w, ((0, 0), (0, 0), (0, HEAD_DIM), (0, 0)))
    hi = jnp.pad(w, ((0, 0), (0, 0), (HEAD_DIM, 0), (0, 0)))
    sel = hi if value_in_upper_half else lo
    out = jnp.stack([sel[0], lo[1], sel[2], sel[3]], axis=0)
    return out.reshape(16, LANES, -1).astype(BF16)


def _t5_bucket(rel):
    n = jnp.maximum(rel, 0)
    max_exact = N_BUCKETS // 2
    nf = jnp.maximum(n, 1).astype(F32)
    large = max_exact + (jnp.log(nf / max_exact) / math.log(MAX_DISTANCE / max_exact)
                         * (N_BUCKETS - max_exact)).astype(I32)
    return jnp.where(n < max_exact, n, jnp.minimum(large, N_BUCKETS - 1))


def _t5_rows(table_cols, rel, reps):
    hot = (_t5_bucket(rel)[..., None] == jnp.arange(N_BUCKETS, dtype=I32)).astype(F32)
    b = jnp.einsum('rck,kh->hrc', hot, table_cols.astype(F32), precision=lax.Precision.HIGHEST)
    b = b.reshape(N_HEADS * rel.shape[0], rel.shape[1])
    return jnp.concatenate([b] * reps, axis=0)


def _rope_tables(pos):
    half = MLA_ROPE // 2
    inv_freq = ROPE_BASE ** (-jnp.arange(half, dtype=F32) / half)
    ang = pos.astype(F32)[:, None] * inv_freq[None, :]
    cos, sin = jnp.cos(ang), jnp.sin(ang)
    cos_t = _pad_last(jnp.concatenate([cos, cos], axis=1), LANES)
    sin_t = _pad_last(jnp.concatenate([-sin, sin], axis=1), LANES)
    return cos_t, sin_t


def _ada_kernel(c_ref, w_ref, b_ref, o_ref):
    c = c_ref[...]
    o_ref[0] = _bdot(c * jax.nn.sigmoid(c), w_ref[0]) + b_ref[0]


def _ada(c_all, w_ada, b_ada):
    depth, d, n = w_ada.shape
    rows = c_all.shape[0]
    tn = n // 4 if n % (4 * LANES) == 0 else n
    return pl.pallas_call(
        _ada_kernel,
        grid=(depth, n // tn),
        in_specs=[pl.BlockSpec((rows, d), lambda l, j: (0, 0)),
                  pl.BlockSpec((1, d, tn), lambda l, j: (l, 0, j)),
                  pl.BlockSpec((1, 1, tn), lambda l, j: (l, 0, j))],
        out_specs=pl.BlockSpec((1, rows, tn), lambda l, j: (l, 0, j)),
        out_shape=jax.ShapeDtypeStruct((depth, rows, n), F32),
        compiler_params=_cparams(("arbitrary", "arbitrary")),
        name="ada_mod",
    )(c_all, w_ada, b_ada.reshape(depth, 1, n))


def _rope_apply(x, cos_t, sin_t):
    lane = lax.broadcasted_iota(I32, x.shape, 1)
    first_half = (lane % MLA_ROPE) < (MLA_ROPE // 2)
    swapped = jnp.where(first_half, pltpu.roll(x, LANES - MLA_ROPE // 2, 1), pltpu.roll(x, MLA_ROPE // 2, 1))
    return x * cos_t + swapped * sin_t


def _proj_kernel(*refs, seq, tr, transposed):
    (x_ref, sc_ref, sh_ref, w_ref, fb_ref, gq_ref, wuq_ref, gkv_ref, wk_ref, cos_ref, sin_ref) = refs[:11]
    out_refs, carry_ref = refs[11:-1], refs[-1]
    r = pl.program_id(0)
    h = x_ref[...] * (1.0 + sc_ref[0]) + sh_ref[0]
    p = _bdot(h, w_ref[...])

    def slab(i, n=1):
        return p[:, i * LANES:(i + n) * LANES]

    lane = lax.broadcasted_iota(I32, (tr, LANES), 1)
    ff = slab(S_FF) + fb_ref[...]
    logf = jnp.minimum(ff, 0.0) - jnp.log(1.0 + jnp.exp(-jnp.abs(ff)))
    logf = jnp.where(lane < N_HEADS, logf, 0.0)

    cq = slab(S_CQ, 2)
    cqn = cq * lax.rsqrt(jnp.sum(cq * cq, axis=-1, keepdims=True) * (1.0 / MLA_Q_RANK) + EPS) * gq_ref[...]
    q_mla = _bdot(cqn, wuq_ref[...])
    cos_t, sin_t = cos_ref[...], sin_ref[...]
    qm_slabs = []
    for hd in range(N_HEADS):
        q_nope = q_mla[:, hd * LANES:(hd + 1) * LANES]
        qm_slabs.append(_bdot(q_nope, wk_ref[hd]))
        q_rope = q_mla[:, (N_HEADS + hd) * LANES:(N_HEADS + hd + 1) * LANES]
        qm_slabs.append(_rope_apply(q_rope, cos_t, sin_t))

    ckv = slab(S_CKV)
    ckvn = ckv * lax.rsqrt(jnp.mean(ckv * ckv, axis=-1, keepdims=True) + EPS) * gkv_ref[...]
    krope = _rope_apply(slab(S_KR), cos_t, sin_t)

    rows_ref = out_refs[0]
    for i, v in enumerate((slab(S_FKV), logf, ckvn, krope, slab(S_DKV), slab(S_SKV), slab(S_IK))):
        rows_ref[:, i * LANES:(i + 1) * LANES] = v

    ti = lax.broadcasted_iota(I32, (tr, tr), 0)
    tj = lax.broadcasted_iota(I32, (tr, tr), 1)
    if tr <= seq:
        tri = tj <= ti
    else:
        tri = (tj <= ti) & ((tj // seq) == (ti // seq))
    cum = jnp.dot(tri.astype(F32), logf, precision=lax.Precision.HIGHEST, preferred_element_type=F32)
    if tr <= seq:
        @pl.when(r % (seq // tr) == 0)
        def _():
            carry_ref[...] = jnp.zeros_like(carry_ref)
        cum = cum + carry_ref[0:1, :]
        carry_ref[...] = jnp.broadcast_to(cum[tr - 1:tr, :], carry_ref.shape)
    out_refs[1][...] = cum

    if not transposed:
        qf_ref, qm_ref, qd_ref, qs_ref, qi_ref, iw_ref = out_refs[2:]
        qf_ref[...] = slab(S_FQ, N_HEADS).astype(BF16)
        for i, v in enumerate(qm_slabs):
            qm_ref[:, i * LANES:(i + 1) * LANES] = v.astype(BF16)
        qd_ref[...] = slab(S_DQ, 2 * N_HEADS).astype(BF16)
        qs_ref[...] = slab(S_SQ, N_HEADS).astype(BF16)
        qi_ref[...] = slab(S_IQ, IDX_HEADS).astype(BF16)
        iw_ref[...] = slab(S_IW)
        return

    kvb_ref, vt_ref, qt_ref, aux_ref = out_refs[2:]
    hi = cum.astype(BF16).astype(F32)
    mid = (cum - hi).astype(BF16).astype(F32)
    lo = cum - hi - mid
    ones = jnp.where((lane >= 3 * N_HEADS) & (lane < 3 * N_HEADS + 3), 1.0, 0.0)
    aug = hi + pltpu.roll(mid, N_HEADS, 1) + pltpu.roll(lo, 2 * N_HEADS, 1) + ones
    for i, v in enumerate((ckvn, krope, slab(S_FKV), aug, slab(S_DKV), aug, slab(S_SKV), slab(S_IK))):
        kvb_ref[:, i * LANES:(i + 1) * LANES] = v.astype(BF16)
    for i, v in enumerate((ckvn, slab(S_FKV), slab(S_DKV), slab(S_SKV))):
        vt_ref[0, i * LANES:(i + 1) * LANES, :] = v.T.astype(BF16)
    q_list = ([v * (MLA_NOPE + MLA_ROPE) ** -0.5 for v in qm_slabs]
              + [slab(S_DQ + i) * DIFF_HALF ** -0.5 for i in range(2 * N_HEADS)]
              + [slab(S_IQ + i) for i in range(IDX_HEADS)]
              + [slab(S_FQ + i) * HEAD_DIM ** -0.5 for i in range(N_HEADS)]
              + [slab(S_SQ + i) * HEAD_DIM ** -0.5 for i in range(N_HEADS)])
    for i, v in enumerate(q_list):
        qt_ref[0, i * LANES:(i + 1) * LANES, :] = v.T.astype(BF16)
    aux_ref[0, 0:LANES, :] = cum.T
    aux_ref[0, LANES:2 * LANES, :] = slab(S_IW).T


def _tile_mod(a, tr):
    return a if a.ndim == 3 else a.reshape(a.shape[0] // tr, tr, a.shape[1])


def _mod_map(mr, tr, rows_per_mod):
    if mr == 1:
        return lambda r, *_: (r * tr // rows_per_mod, 0, 0)
    return lambda r, *_: (r, 0, 0)


def _project(x2d, scale_b, shift_b, wts, cos_t, sin_t, seq, rows_per_mod, transposed):
    rtot, d = x2d.shape
    tr = min(ROW_TILE, rtot)
    assert rtot % tr == 0 and (seq % tr == 0 or tr % seq == 0)
    scale_b, shift_b = _tile_mod(scale_b, tr), _tile_mod(shift_b, tr)
    mr = scale_b.shape[1]
    mod_map = _mod_map(mr, tr, rows_per_mod)
    full = lambda a: pl.BlockSpec(a.shape, lambda r: (0,) * a.ndim)
    row = lambda n: pl.BlockSpec((tr, n), lambda r: (r, 0))
    outs = [(7 * LANES, F32), (LANES, F32)]
    if transposed:
        outs += [(8 * LANES, BF16)]
        touts = [(4 * LANES, BF16), (32 * LANES, BF16), (2 * LANES, F32)]
    else:
        outs += [(4 * LANES, BF16), (8 * LANES, BF16), (8 * LANES, BF16), (4 * LANES, BF16), (8 * LANES, BF16),
                 (LANES, F32)]
        touts = []
    return pl.pallas_call(
        functools.partial(_proj_kernel, seq=seq, tr=tr, transposed=transposed),
        grid=(rtot // tr,),
        in_specs=[row(d), pl.BlockSpec((1, mr, d), mod_map), pl.BlockSpec((1, mr, d), mod_map),
                  full(wts['w_in_p']), full(wts['fb']), full(wts['gq']), full(wts['wuq_p']), full(wts['gkv']),
                  full(wts['wk']), row(LANES), row(LANES)],
        out_specs=[row(n) for n, _ in outs] + [pl.BlockSpec((1, n, tr), lambda r: (r, 0, 0)) for n, _ in touts],
        out_shape=[jax.ShapeDtypeStruct((rtot, n), dt) for n, dt in outs]
        + [jax.ShapeDtypeStruct((rtot // tr, n, tr), dt) for n, dt in touts],
        scratch_shapes=[pltpu.VMEM((8, LANES), F32)],
        compiler_params=_cparams(("arbitrary",)),
        name="in_proj",
    )(x2d, scale_b, shift_b, wts['w_in_p'], wts['fb'], wts['gq'], wts['wuq_p'], wts['gkv'], wts['wk'],
      cos_t, sin_t)


def _col_online(st, vt, m_ref, l_ref, acc_ref, keep=None):
    m_prev = m_ref[...]
    m_new = jnp.maximum(m_prev, jnp.max(st, axis=0, keepdims=True))
    alpha = jnp.exp(m_prev - m_new)
    p = jnp.exp(st - m_new)
    if keep is not None:
        p = jnp.where(keep, p, 0.0)
    l_ref[...] = alpha * l_ref[...] + jnp.sum(p, axis=0, keepdims=True)
    acc_ref[...] = alpha * acc_ref[...] + jnp.dot(vt, p.astype(BF16), preferred_element_type=F32)
    m_ref[...] = m_new


def _causal_t(tk, n, tq):
    si = lax.broadcasted_iota(I32, (tk, n), 0)
    ti = lax.broadcasted_iota(I32, (tk, n), 1) % tq
    return si <= ti


def _store_heads_t(o_ref, ot, tq):
    for h in range(N_HEADS):
        o_ref[h] = ot[:, h * tq:(h + 1) * tq].T.astype(BF16)


def _init_state(m_ref, l_ref, acc_ref):
    m_ref[...] = jnp.full(m_ref.shape, NEG_INF, F32)
    l_ref[...] = jnp.zeros(l_ref.shape, F32)
    acc_ref[...] = jnp.zeros(acc_ref.shape, F32)


def _split3(x):
    hi = x.astype(BF16).astype(F32)
    mid = (x - hi).astype(BF16).astype(F32)
    return hi, mid, x - hi - mid


def _const_rows(row):
    hi, mid, lo = _split3(row)
    ri = lax.broadcasted_iota(I32, (LANES, row.shape[1]), 0)
    blk = jnp.where(ri == 3 * N_HEADS, hi, 0.0)
    blk = jnp.where(ri == 3 * N_HEADS + 1, mid, blk)
    blk = jnp.where(ri == 3 * N_HEADS + 2, lo, blk)
    return blk.astype(BF16)


def _fox_kernel(qt_ref, aux_ref, k_ref, vt_ref, o_ref, qs_ref, m_ref, l_ref, acc_ref, *, tq):
    i = pl.program_id(1)
    ri = lax.broadcasted_iota(I32, (LANES, tq), 0)
    for h in range(N_HEADS):
        qs_ref[0:LANES, h * tq:(h + 1) * tq] = qt_ref[h * LANES:(h + 1) * LANES, :]
        hi, mid, lo = _split3(aux_ref[h:h + 1, :])
        blk = jnp.where((ri < 3 * N_HEADS) & (ri % N_HEADS == h), -1.0, 0.0)
        blk = jnp.where(ri == 3 * N_HEADS, hi, blk)
        blk = jnp.where(ri == 3 * N_HEADS + 1, mid, blk)
        blk = jnp.where(ri == 3 * N_HEADS + 2, lo, blk)
        qs_ref[LANES:2 * LANES, h * tq:(h + 1) * tq] = blk.astype(BF16)
    _init_state(m_ref, l_ref, acc_ref)

    def step(j, diag):
        k = k_ref[pl.ds(pl.multiple_of(j * tq, tq), tq), :]
        st = jnp.dot(k, qs_ref[...], preferred_element_type=F32)
        if diag:
            st = jnp.where(_causal_t(tq, N_HEADS * tq, tq), st, NEG_INF)
        _col_online(st, vt_ref[j], m_ref, l_ref, acc_ref)

    def body(j, c):
        step(j, False)
        return c

    lax.fori_loop(0, i, body, 0)
    step(i, True)
    _store_heads_t(o_ref, acc_ref[...] / l_ref[...], tq)


def _mla_kernel(qt_ref, k_ref, vt_ref, o_ref, qs_ref, m_ref, l_ref, acc_ref, *, tq):
    i = pl.program_id(1)
    for h in range(N_HEADS):
        qs_ref[:, h * tq:(h + 1) * tq] = qt_ref[h * 2 * LANES:(h + 1) * 2 * LANES, :]
    _init_state(m_ref, l_ref, acc_ref)

    def step(j, diag):
        k = k_ref[pl.ds(pl.multiple_of(j * tq, tq), tq), :]
        st = jnp.dot(k, qs_ref[...], preferred_element_type=F32)
        if diag:
            st = jnp.where(_causal_t(tq, N_HEADS * tq, tq), st, NEG_INF)
        _col_online(st, vt_ref[j], m_ref, l_ref, acc_ref)

    def body(j, c):
        step(j, False)
        return c

    lax.fori_loop(0, i, body, 0)
    step(i, True)
    _store_heads_t(o_ref, acc_ref[...] / l_ref[...], tq)


def _near_far_loop(i, step):
    def body(j, c):
        step(j, 2)
        return c

    lax.fori_loop(0, jnp.maximum(i - 1, 0), body, 0)

    @pl.when(i >= 1)
    def _():
        step(i - 1, 1)

    step(i, 0)


def _diff_kernel(lam_ref, qt_ref, k_ref, vt_ref, tb_ref, tfar_ref, g_ref, o_ref, qs_ref, m_ref, l_ref, acc_ref,
                 *, tq, lam_init):
    i = pl.program_id(1)
    nq = N_HEADS * tq
    for g in range(2):
        for h in range(N_HEADS):
            qs_ref[0:LANES, (g * N_HEADS + h) * tq:(g * N_HEADS + h + 1) * tq] = \
                qt_ref[(2 * h + g) * LANES:(2 * h + g + 1) * LANES, :]
    qs_ref[LANES:2 * LANES, :] = _const_rows(tfar_ref[...])
    _init_state(m_ref, l_ref, acc_ref)

    def step(j, dist):
        k = k_ref[pl.ds(pl.multiple_of(j * tq, tq), tq), :]
        st = jnp.dot(k, qs_ref[...], preferred_element_type=F32)
        if dist < 2:
            st = st + tb_ref[dist]
        if dist == 0:
            st = jnp.where(_causal_t(tq, 2 * nq, tq), st, NEG_INF)
        _col_online(st, vt_ref[j], m_ref, l_ref, acc_ref)

    _near_far_loop(i, step)
    ot = acc_ref[...] / l_ref[...]
    ot = ot[:, :nq] - lam_ref[0] * ot[:, nq:]
    ri = lax.broadcasted_iota(I32, ot.shape, 0)
    ot = jnp.where(ri >= HEAD_DIM, ot, 0.0)
    y = ot * lax.rsqrt(jnp.sum(ot * ot, axis=0, keepdims=True) * (1.0 / HEAD_DIM) + EPS) * g_ref[...]
    _store_heads_t(o_ref, y * (1.0 - lam_init), tq)


def _score_keys(sc):
    bits = lax.bitcast_convert_type(sc, I32)
    key = jnp.where(bits < 0, bits ^ jnp.int32(0x7FFFFFFF), bits)
    return jnp.where(sc == 0.0, 0, key)


def _dsa_kernel(qit_ref, iwt_ref, ki_ref, qt_ref, k_ref, vt_ref, tb_ref, tfar_ref, o_ref,
                keys_ref, qis_ref, wr_ref, thr_ref, qs_ref, m_ref, l_ref, acc_ref, *, tq, nsel):
    i = pl.program_id(1)
    for h in range(IDX_HEADS):
        qis_ref[:, h * tq:(h + 1) * tq] = qit_ref[h * LANES:(h + 1) * LANES, :]
        wr_ref[:, h * tq:(h + 1) * tq] = iwt_ref[h:h + 1, :] * (IDX_HEADS ** -0.5 * IDX_DIM ** -0.5)
    for h in range(N_HEADS):
        qs_ref[:, h * tq:(h + 1) * tq] = qt_ref[h * LANES:(h + 1) * LANES, :]
    _init_state(m_ref, l_ref, acc_ref)

    def score_tile(j, diag):
        kt = ki_ref[pl.ds(pl.multiple_of(j * tq, tq), tq), :]
        r = jnp.maximum(jnp.dot(kt, qis_ref[...], preferred_element_type=F32), 0.0) * wr_ref[...]
        sc = r[:, 0:tq]
        for h in range(1, IDX_HEADS):
            sc = sc + r[:, h * tq:(h + 1) * tq]
        key = _score_keys(sc)
        if diag:
            key = jnp.where(_causal_t(tq, tq, tq), key, INT_MIN)
        keys_ref[j] = key

    def sbody(j, c):
        score_tile(j, False)
        return c

    lax.fori_loop(0, i, sbody, 0)
    score_tile(i, True)

    def count_ge(cand):
        def cbody(j, c):
            ge = jnp.where(keys_ref[j] >= cand, 1.0, 0.0)
            return c + jnp.sum(ge.reshape(tq // 8, 8, tq), axis=0)

        c = lax.fori_loop(0, i + 1, cbody, jnp.zeros((8, tq), F32))
        return jnp.sum(c, axis=0, keepdims=True)

    zero = jnp.zeros((1, tq), I32)
    t0 = jnp.where(count_ge(zero) >= nsel, zero, INT_MIN)

    def bit_body(b, t):
        cand = t + jnp.left_shift(jnp.int32(1), 30 - b)
        return jnp.where(count_ge(cand) >= nsel, cand, t)

    thr_ref[...] = lax.fori_loop(0, 31, bit_body, t0)

    def step(j, dist):
        k = k_ref[pl.ds(pl.multiple_of(j * tq, tq), tq), :]
        st = jnp.dot(k, qs_ref[...], preferred_element_type=F32) + tfar_ref[...]
        if dist < 2:
            st = st + tb_ref[dist]
        sel = keys_ref[j] >= thr_ref[...]
        if dist == 0:
            sel = sel & _causal_t(tq, tq, tq)
        self32 = jnp.where(sel, 1.0, 0.0)
        keep = jnp.concatenate([self32] * N_HEADS, axis=1) > 0.5
        st = jnp.where(keep, st, NEG_INF)
        _col_online(st, vt_ref[j], m_ref, l_ref, acc_ref, keep=keep)

    _near_far_loop(i, step)
    _store_heads_t(o_ref, acc_ref[...] / l_ref[...], tq)


def _prompt_attention(pr, t5_table, lam, lam_init, subln, bsz, seq):
    _, _, kvb, vt, qt, aux = pr
    tq = min(Q_TILE, seq)
    assert seq % tq == 0 and tq == min(ROW_TILE, bsz * seq)
    nt = seq // tq
    kvb3 = kvb.reshape(bsz, seq, kvb.shape[-1])
    vt4 = vt.reshape(bsz, nt, vt.shape[1], tq)

    qspec = lambda n, blk: pl.BlockSpec((None, n * LANES, tq), lambda b, i: (b * nt + i, blk, 0))
    auxspec = lambda blk: pl.BlockSpec((None, LANES, tq), lambda b, i: (b * nt + i, blk, 0))
    kspec = lambda n, blk: pl.BlockSpec((None, seq, n * LANES), lambda b, i: (b, 0, blk))
    vspec = lambda blk: pl.BlockSpec((None, nt, LANES, tq), lambda b, i: (b, 0, blk, 0))
    ospec = pl.BlockSpec((None, N_HEADS, tq, LANES), lambda b, i: (b, 0, i, 0))
    oshape = jax.ShapeDtypeStruct((bsz, N_HEADS, seq, LANES), BF16)
    rowv = lambda n: pltpu.VMEM((1, n), F32)
    grid = (bsz, nt)
    sem = ("arbitrary", "arbitrary")
    nq = N_HEADS * tq

    o_fox = pl.pallas_call(
        functools.partial(_fox_kernel, tq=tq), grid=grid,
        in_specs=[qspec(4, 6), auxspec(0), kspec(2, 1), vspec(1)],
        out_specs=ospec, out_shape=oshape,
        scratch_shapes=[pltpu.VMEM((2 * LANES, nq), BF16), rowv(nq), rowv(nq), pltpu.VMEM((LANES, nq), F32)],
        compiler_params=_cparams(sem), name="fox_attention",
    )(qt, aux, kvb3, vt4)

    o_mla = pl.pallas_call(
        functools.partial(_mla_kernel, tq=tq), grid=grid,
        in_specs=[qspec(8, 0), kspec(2, 0), vspec(0)],
        out_specs=ospec, out_shape=oshape,
        scratch_shapes=[pltpu.VMEM((2 * LANES, nq), BF16), rowv(nq), rowv(nq), pltpu.VMEM((LANES, nq), F32)],
        compiler_params=_cparams(sem), name="mla_attention",
    )(qt, kvb3, vt4)

    ri = jnp.arange(tq, dtype=I32)
    rel0 = ri[:, None] - ri[None, :]
    far_rel = jnp.full((1, 1), 2 * MAX_DISTANCE, I32)

    def t5_tiles(cols, reps):
        tb = jnp.stack([_t5_rows(cols, rel0, reps).T, _t5_rows(cols, rel0 + tq, reps).T], axis=0)
        tfar = jnp.repeat(_t5_rows(cols, far_rel, reps), tq, axis=0).T
        return (tb - tfar[None]).astype(F32), tfar.astype(F32)

    assert tq >= MAX_DISTANCE
    tb_d, tfar_d = t5_tiles(t5_table[:, :N_HEADS], 2)
    tb_s, tfar_s = t5_tiles(t5_table[:, N_HEADS:], 1)
    full = lambda a: pl.BlockSpec(a.shape, lambda b, i: (0,) * a.ndim)
    gcol = subln.T

    o_diff = pl.pallas_call(
        functools.partial(_diff_kernel, tq=tq, lam_init=lam_init), grid=grid,
        in_specs=[pl.BlockSpec(memory_space=pltpu.SMEM), qspec(8, 1), kspec(2, 2), vspec(2),
                  full(tb_d), full(tfar_d), full(gcol)],
        out_specs=ospec, out_shape=oshape,
        scratch_shapes=[pltpu.VMEM((2 * LANES, 2 * nq), BF16), rowv(2 * nq), rowv(2 * nq),
                        pltpu.VMEM((LANES, 2 * nq), F32)],
        compiler_params=_cparams(sem), name="diff_attention",
    )(lam.reshape(1), qt, kvb3, vt4, tb_d, tfar_d, gcol)

    nsel = min(IDX_TOPK, seq // 4)
    o_dsa = pl.pallas_call(
        functools.partial(_dsa_kernel, tq=tq, nsel=nsel), grid=grid,
        in_specs=[qspec(8, 2), auxspec(1), kspec(1, 7), qspec(4, 7), kspec(1, 6), vspec(3),
                  full(tb_s), full(tfar_s)],
        out_specs=ospec, out_shape=oshape,
        scratch_shapes=[pltpu.VMEM((nt, tq, tq), I32), pltpu.VMEM((LANES, IDX_HEADS * tq), BF16),
                        rowv(IDX_HEADS * tq), pltpu.VMEM((1, tq), I32),
                        pltpu.VMEM((LANES, nq), BF16), rowv(nq), rowv(nq), pltpu.VMEM((LANES, nq), F32)],
        compiler_params=_cparams(sem), name="dsa_attention",
    )(qt, aux, kvb3, qt, kvb3, vt4, tb_s, tfar_s)
    return o_fox, o_mla, o_diff, o_dsa


def _out_kernel(x_ref, of_ref, om_ref, od_ref, os_ref, wo_ref, wv_ref, g1_ref, sc2_ref, sh2_ref,
                lng_ref, lnb_ref, rw_ref, rb_ref, x1_ref, h2_ref, ti_ref, tg_ref, rk_ref, cnt_ref, cnt_sc,
                *, tr, alpha):
    d = x_ref.shape[-1]
    acc = jnp.zeros((tr, d), F32)
    for h in range(N_HEADS):
        acc = acc + _bdot(of_ref[:, h].reshape(tr, LANES), wo_ref[h])
        o_mla = _bdot(om_ref[:, h].reshape(tr, LANES), wv_ref[h])
        acc = acc + _bdot(o_mla, wo_ref[N_HEADS + h])
        acc = acc + _bdot(od_ref[:, h].reshape(tr, LANES), wo_ref[2 * N_HEADS + h])
        acc = acc + _bdot(os_ref[:, h].reshape(tr, LANES), wo_ref[3 * N_HEADS + h])
    x1 = _layer_norm(alpha * x_ref[...] + g1_ref[0] * acc, lng_ref[...], lnb_ref[...])
    x1_ref[...] = x1
    h2 = x1 * (1.0 + sc2_ref[0]) + sh2_ref[0]
    h2_ref[...] = h2
    logits = jnp.dot(h2, rw_ref[...], precision=lax.Precision.HIGHEST, preferred_element_type=F32) + rb_ref[...]
    lane = lax.broadcasted_iota(I32, (tr, LANES), 1).astype(F32)
    idx_out = jnp.zeros((tr, LANES), F32)
    vals, hots = [], []
    for k in range(TOP_K):
        mx = jnp.max(logits, axis=1, keepdims=True)
        ix = jnp.min(jnp.where(logits == mx, lane, float(LANES)), axis=1, keepdims=True)
        idx_out = jnp.where(lane == k, ix, idx_out)
        vals.append(mx)
        hots.append(lane == ix)
        logits = jnp.where(hots[-1], -3.4e38, logits)
    @pl.when(pl.program_id(0) == 0)
    def _():
        cnt_sc[...] = jnp.zeros_like(cnt_sc)
    picked = sum(jnp.where(hh, 1.0, 0.0) for hh in hots)
    ti_ = lax.broadcasted_iota(I32, (tr, tr), 0)
    tj_ = lax.broadcasted_iota(I32, (tr, tr), 1)
    before = _bdot(jnp.where(tj_ < ti_, 1.0, 0.0), picked) + cnt_sc[0:1, :]
    rank_out = jnp.zeros((tr, LANES), F32)
    for k in range(TOP_K):
        rk = jnp.sum(jnp.where(hots[k], before, 0.0), axis=1, keepdims=True)
        rank_out = jnp.where(lane == k, rk, rank_out)
    rk_ref[...] = rank_out.astype(I32)
    cnt_new = cnt_sc[...] + jnp.sum(picked, axis=0, keepdims=True)
    cnt_sc[...] = cnt_new
    cnt_ref[...] = cnt_new
    es = [jnp.exp(v - vals[0]) for v in vals]
    den = es[0] + es[1] + es[2] + es[3]
    gates = jnp.zeros((tr, LANES), F32)
    for k in range(TOP_K):
        gates = jnp.where(lane == k, es[k] / den, gates)
    ti_ref[...] = idx_out.astype(I32)
    tg_ref[...] = gates


def _out_router(x2d, o_list, wo, wv, gate1, scale2, shift2, ln_g, ln_b, rw, rb, rows_per_mod, alpha):
    rtot, d = x2d.shape
    tr = min(ROW_TILE, rtot)
    nb_o, _, tt, _ = o_list[0].shape
    gate1, scale2, shift2 = _tile_mod(gate1, tr), _tile_mod(scale2, tr), _tile_mod(shift2, tr)
    mr = gate1.shape[1]
    mod_map = _mod_map(mr, tr, rows_per_mod)
    if tt >= tr:
        assert tt % tr == 0
        per = tt // tr
        ospec = pl.BlockSpec((1, N_HEADS, tr, LANES), lambda r: (r // per, 0, r % per, 0))
    else:
        assert tr % tt == 0
        ospec = pl.BlockSpec((tr // tt, N_HEADS, tt, LANES), lambda r: (r, 0, 0, 0))
    full = lambda a: pl.BlockSpec(a.shape, lambda r: (0,) * a.ndim)
    row = lambda n: pl.BlockSpec((tr, n), lambda r: (r, 0))
    mspec = pl.BlockSpec((1, mr, d), mod_map)
    return pl.pallas_call(
        functools.partial(_out_kernel, tr=tr, alpha=alpha),
        grid=(rtot // tr,),
        in_specs=[row(d), ospec, ospec, ospec, ospec, full(wo), full(wv), mspec, mspec, mspec,
                  full(ln_g), full(ln_b), full(rw), full(rb)],
        out_specs=[row(d), row(d), row(LANES), row(LANES), row(LANES), pl.BlockSpec((8, LANES), lambda r: (0, 0))],
        out_shape=[jax.ShapeDtypeStruct((rtot, d), F32), jax.ShapeDtypeStruct((rtot, d), F32),
                   jax.ShapeDtypeStruct((rtot, LANES), I32), jax.ShapeDtypeStruct((rtot, LANES), F32),
                   jax.ShapeDtypeStruct((rtot, LANES), I32), jax.ShapeDtypeStruct((8, LANES), F32)],
        scratch_shapes=[pltpu.VMEM((8, LANES), F32)],
        compiler_params=_cparams(("arbitrary",)),
        name="out_proj_router",
    )(x2d, *o_list, wo, wv, gate1, scale2, shift2, ln_g, ln_b, rw, rb)


def _routing_tables(top_idx, rank, counts, n_tok):
    a = n_tok * TOP_K
    e_flat = top_idx.reshape(a)
    padded = (counts + EXPERT_BLOCK - 1) // EXPERT_BLOCK * EXPERT_BLOCK
    pstart = jnp.cumsum(padded) - padded
    dest = (pstart[e_flat] + rank.reshape(a)).astype(I32)
    n_blocks = -(-a // EXPERT_BLOCK) + N_EXPERTS
    src = jnp.zeros((n_blocks * EXPERT_BLOCK,), I32).at[dest].set(jnp.arange(a, dtype=I32) // TOP_K)
    ends = pstart + padded
    first_row = jnp.arange(n_blocks, dtype=I32) * EXPERT_BLOCK
    blk_expert = jnp.minimum(jnp.sum((ends[None, :] <= first_row[:, None]).astype(I32), axis=1), N_EXPERTS - 1)
    return dest, src, blk_expert, n_blocks


def _moe_kernel(be_ref, src_ref, h_hbm, wgu_ref, bgu_ref, wdn_ref, bdn_ref, y_ref,
                xbuf0, xbuf1, xbuf2, sem, wgu_b, wdn_b, *, nblk, dff):
    i = pl.program_id(0)
    slot = i % 3

    bufs = (xbuf0, xbuf1, xbuf2)

    def row_copy(blk, sl, r):
        tok = src_ref[blk * EXPERT_BLOCK + r]
        return pltpu.make_async_copy(h_hbm.at[pl.ds(tok, 1)], bufs[sl].at[pl.ds(r, 1)], sem.at[sl])

    def wait_rows(blk, sl):
        del blk
        pltpu.make_async_copy(h_hbm.at[pl.ds(0, EXPERT_BLOCK)], bufs[sl], sem.at[sl]).wait()

    @pl.when(i == 0)
    def _():
        def body(r, c):
            row_copy(0, 0, r).start()
            row_copy(min(1, nblk - 1), 1, r).start()
            return c
        lax.fori_loop(0, EXPERT_BLOCK, body, 0)

    prev = be_ref[jnp.maximum(i - 1, 0)]

    @pl.when((i == 0) | (be_ref[i] != prev))
    def _():
        wgu_b[...] = wgu_ref[0, 0].astype(BF16)
        wdn_b[...] = wdn_ref[0, 0].astype(BF16)

    nxt = jnp.minimum(i + 2, nblk - 1)
    for par in range(3):
        @pl.when(slot == par)
        def _(par=par):
            wait_rows(i, par)
            for r in range(EXPERT_BLOCK):
                row_copy(nxt, (par + 2) % 3, r).start()
            gu = jnp.dot(bufs[par][...].astype(BF16), wgu_b[...], preferred_element_type=F32) + bgu_ref[0, 0]
            g = jnp.minimum(gu[:, :dff], SWIGLU_LIMIT)
            u = jnp.clip(gu[:, dff:], -SWIGLU_LIMIT, SWIGLU_LIMIT)
            act = g * jax.nn.sigmoid(SWIGLU_ALPHA * g) * (u + 1.0)
            y_ref[...] = jnp.dot(act.astype(BF16), wdn_b[...], preferred_element_type=F32) + bdn_ref[0, 0]

            @pl.when(i == nblk - 1)
            def _():
                wait_rows(nxt, (par + 1) % 3)
                wait_rows(nxt, (par + 2) % 3)


def _moe_blocks(l, h2, blk_expert, src, n_blocks, exp_w_gu, exp_b_gu, exp_w_dn, exp_b_dn):
    rtot, d = h2.shape
    dff = exp_w_dn.shape[2]
    bgu = exp_b_gu.reshape(exp_b_gu.shape[0], N_EXPERTS, 1, 2 * dff)
    bdn = exp_b_dn.reshape(exp_b_dn.shape[0], N_EXPERTS, 1, d)
    gs = pltpu.PrefetchScalarGridSpec(
        num_scalar_prefetch=2,
        grid=(n_blocks,),
        in_specs=[pl.BlockSpec(memory_space=pl.ANY),
                  pl.BlockSpec((1, 1, d, 2 * dff), lambda i, be, s: (l, be[i], 0, 0)),
                  pl.BlockSpec((1, 1, 1, 2 * dff), lambda i, be, s: (l, be[i], 0, 0)),
                  pl.BlockSpec((1, 1, dff, d), lambda i, be, s: (l, be[i], 0, 0)),
                  pl.BlockSpec((1, 1, 1, d), lambda i, be, s: (l, be[i], 0, 0))],
        out_specs=pl.BlockSpec((EXPERT_BLOCK, d), lambda i, be, s: (i, 0)),
        scratch_shapes=[pltpu.VMEM((EXPERT_BLOCK, d), F32)] * 3 + [
                        pltpu.SemaphoreType.DMA((3,)),
                        pltpu.VMEM((d, 2 * dff), BF16), pltpu.VMEM((dff, d), BF16)])
    return pl.pallas_call(
        functools.partial(_moe_kernel, nblk=n_blocks, dff=dff),
        grid_spec=gs,
        out_shape=jax.ShapeDtypeStruct((n_blocks * EXPERT_BLOCK, d), F32),
        compiler_params=_cparams(("arbitrary",)),
        name="moe_experts",
    )(blk_expert, src, h2, exp_w_gu, bgu, exp_w_dn, bdn)


def _final_kernel(dest_ref, x1_ref, y_hbm, tg_ref, g2_ref, lng_ref, lnb_ref, o_ref, ybuf, sem,
                  *, nsteps, tr, alpha):
    i = pl.program_id(0)
    slot = i % 2

    def row_copy(step, sl, r, k):
        pos = dest_ref[(step * tr + r) * TOP_K + k]
        return pltpu.make_async_copy(y_hbm.at[pl.ds(pos, 1)], ybuf.at[sl, k, pl.ds(r, 1)], sem.at[sl])

    def issue(step, sl):
        def body(r, c):
            for k in range(TOP_K):
                row_copy(step, sl, r, k).start()
            return c
        lax.fori_loop(0, tr, body, 0)

    @pl.when(i == 0)
    def _():
        issue(0, 0)

    @pl.when(i + 1 < nsteps)
    def _():
        issue(i + 1, 1 - slot)

    for k in range(TOP_K):
        pltpu.make_async_copy(y_hbm.at[pl.ds(0, tr)], ybuf.at[slot, k], sem.at[slot]).wait()

    tg = tg_ref[...]
    y = tg[:, 0:1] * ybuf[slot, 0]
    for k in range(1, TOP_K):
        y = y + tg[:, k:k + 1] * ybuf[slot, k]
    o_ref[...] = _layer_norm(alpha * x1_ref[...] + g2_ref[0] * y, lng_ref[...], lnb_ref[...])


def _combine_final(x1, y_pad, dest, gates, gate2, ln_g, ln_b, rows_per_mod, alpha):
    rtot, d = x1.shape
    tr = min(FINAL_TILE, rtot)
    nsteps = rtot // tr
    gate2 = _tile_mod(gate2, tr)
    mod_map = _mod_map(gate2.shape[1], tr, rows_per_mod)
    gs = pltpu.PrefetchScalarGridSpec(
        num_scalar_prefetch=1,
        grid=(nsteps,),
        in_specs=[pl.BlockSpec((tr, d), lambda r, dst: (r, 0)),
                  pl.BlockSpec(memory_space=pl.ANY),
                  pl.BlockSpec((tr, LANES), lambda r, dst: (r, 0)),
                  pl.BlockSpec((1, gate2.shape[1], d), mod_map),
                  pl.BlockSpec(ln_g.shape, lambda r, dst: (0, 0)),
                  pl.BlockSpec(ln_b.shape, lambda r, dst: (0, 0))],
        out_specs=pl.BlockSpec((tr, d), lambda r, dst: (r, 0)),
        scratch_shapes=[pltpu.VMEM((2, TOP_K, tr, d), F32), pltpu.SemaphoreType.DMA((2,))])
    return pl.pallas_call(
        functools.partial(_final_kernel, nsteps=nsteps, tr=tr, alpha=alpha),
        grid_spec=gs,
        out_shape=jax.ShapeDtypeStruct((rtot, d), F32),
        compiler_params=_cparams(("arbitrary",)),
        name="moe_combine_norm",
    )(dest, x1, y_pad, gates, gate2, ln_g, ln_b)


def _chunk_copies(specs, sem, layer, pt_ref, b, chunk, slot, pages):
    cps = []
    for ci, (cache, buf, placement) in enumerate(specs):
        for p in range(pages):
            page = pt_ref[b, chunk * pages + p]
            src = cache.at[layer, page]
            if placement == 'rows':
                dst = buf.at[slot, pl.ds(p * LANES, LANES)]
            elif placement == 'page':
                dst = buf.at[slot, :, p, :]
            else:
                idx = (slot,) + (slice(None),) * (len(buf.shape) - 2) + (pl.ds(p * LANES, LANES),)
                dst = buf.at[idx]
            cps.append(pltpu.make_async_copy(src, dst, sem.at[slot, ci]))
    return cps


def _pipeline_fetch(specs, sem, layer, pt_ref, nch, pages, chunk_of):
    b, c = pl.program_id(0), pl.program_id(1)
    nb = pl.num_programs(0)
    g = b * nch + c
    slot = g % 2

    @pl.when(g == 0)
    def _():
        for cp in _chunk_copies(specs, sem, layer, pt_ref, b, chunk_of(c), slot, pages):
            cp.start()

    @pl.when(g + 1 < nb * nch)
    def _():
        g1 = g + 1
        b1, c1 = g1 // nch, g1 % nch
        for cp in _chunk_copies(specs, sem, layer, pt_ref, b1, chunk_of(c1), 1 - slot, pages):
            cp.start()

    for cp in _chunk_copies(specs, sem, layer, pt_ref, b, chunk_of(c), slot, pages):
        cp.wait()
    return slot


def _stack_slabs(x, n, width=1):
    return jnp.concatenate([x[:, h * width * LANES:(h + 1) * width * LANES] for h in range(n)], axis=0)


def _dec_idx_kernel(pt_ref, kidx_hbm, logf_hbm, qi_ref, iw_ref, knew_ref, kp_ref, kn_ref, thr_ref, fb_ref,
                    kbuf, lbuf, sem, allk, newk, *, layer, nch, pages, dec, nsel):
    c = pl.program_id(1)
    slot = _pipeline_fetch([(kidx_hbm, kbuf, 'lanes'), (logf_hbm, lbuf, 'page')], sem, layer, pt_ref, nch, pages,
                           lambda cc: cc)
    pj = lax.broadcasted_iota(I32, (LANES, LANES), 0)
    ps = lax.broadcasted_iota(I32, (LANES, LANES), 1)
    within = jnp.where(pj >= ps, 1.0, 0.0)
    gi = lax.broadcasted_iota(I32, (pages, pages), 0)
    gj = lax.broadcasted_iota(I32, (pages, pages), 1)
    later = jnp.where(gj > gi, 1.0, 0.0)
    hp = lax.Precision.HIGHEST
    for h in range(N_HEADS):
        x = lbuf[slot, h]
        incl = jnp.dot(x, within, precision=hp, preferred_element_type=F32)
        tot = jnp.broadcast_to(incl[:, 0:1], (pages, LANES))
        fb_ref[h] = incl - x + jnp.dot(later, tot, precision=hp, preferred_element_type=F32)

    q = _stack_slabs(qi_ref[...], IDX_HEADS)[:, :IDX_DIM]
    iw = iw_ref[...]
    wcol = jnp.concatenate([iw[:, h:h + 1] for h in range(IDX_HEADS)], axis=0)
    wcol = wcol * (IDX_HEADS ** -0.5 * IDX_DIM ** -0.5)

    def scores(kt):
        r = jnp.maximum(_bdot(q, kt), 0.0) * wcol
        sc = r[0:dec]
        for h in range(1, IDX_HEADS):
            sc = sc + r[h * dec:(h + 1) * dec]
        return _score_keys(sc)

    key = scores(kbuf[slot])
    kp_ref[...] = key
    allk[c] = key

    @pl.when(c == nch - 1)
    def _():
        kn = scores(knew_ref[...])
        t = lax.broadcasted_iota(I32, kn.shape, 0)
        s = lax.broadcasted_iota(I32, kn.shape, 1)
        kn = jnp.where(s <= t, kn, INT_MIN)
        kn_ref[...] = kn
        newk[...] = kn

        def count_ge(cand):
            def cbody(j, acc):
                ge = jnp.where(allk[j] >= cand, 1.0, 0.0)
                parts = [ge[:, i * LANES:(i + 1) * LANES] for i in range(ge.shape[1] // LANES)]
                while len(parts) > 1:
                    parts = [a + b for a, b in zip(parts[0::2], parts[1::2])] + (parts[-1:] if len(parts) % 2 else [])
                return acc + parts[0]
            acc = lax.fori_loop(0, nch, cbody, jnp.where(newk[...] >= cand, 1.0, 0.0))
            return jnp.sum(acc, axis=1, keepdims=True)

        zero = jnp.zeros((dec, 1), I32)
        t0 = jnp.where(count_ge(zero) >= nsel, zero, INT_MIN)

        def bit_body(bi, tcur):
            cand = tcur + jnp.left_shift(jnp.int32(1), 30 - bi)
            return jnp.where(count_ge(cand) >= nsel, cand, tcur)

        thr = lax.fori_loop(0, 31, bit_body, t0)
        thr_ref[...] = jnp.broadcast_to(thr, thr_ref.shape)


def _suffix_sum(x):
    n = x.shape[1]
    lane = lax.broadcasted_iota(I32, x.shape, 1)
    d = 1
    while d < n:
        x = x + jnp.where(lane + d < n, pltpu.roll(x, n - d, 1), 0.0)
        d *= 2
    return x


def _online_update_fn(s, pv, m_ref, l_ref, acc_ref, keep=None):
    m_prev = m_ref[...]
    m_new = jnp.maximum(m_prev, jnp.max(s, axis=1, keepdims=True))
    alpha = jnp.exp(m_prev - m_new)
    p = jnp.exp(s - m_new)
    if keep is not None:
        p = jnp.where(keep, p, 0.0)
    l_ref[...] = alpha * l_ref[...] + jnp.sum(p, axis=1, keepdims=True)
    acc_ref[...] = alpha * acc_ref[...] + pv(p)
    m_ref[...] = m_new


def _dec_att_kernel(pt_ref, lam_ref,
                    fox_hbm, ckv_hbm, kr_hbm, diff_hbm, dsa_hbm,
                    qf_ref, qm_ref, qd_ref, qs_ref, ecol_ref, fb_ref, kp_ref, kn_ref, thr_ref,
                    nf_ref, nl_ref, nc_ref, nk_ref, nd_ref, ns_ref,
                    tbd_ref, tbs_ref, tnd_ref, tns_ref, g_ref,
                    of_ref, om_ref, od_ref, os_ref,
                    fbuf, cbuf, kbuf, dbuf, sbuf, sem,
                    mf, lf_, af, mm, lm, am, md, ld, ad, ms, ls, as_, rcar,
                    *, layer, nch, pages, dec, lam_init):
    c = pl.program_id(1)
    specs = [(fox_hbm, fbuf, 'lanes'), (ckv_hbm, cbuf, 'rows'), (kr_hbm, kbuf, 'lanes'),
             (diff_hbm, dbuf, 'lanes'), (dsa_hbm, sbuf, 'lanes')]
    slot = _pipeline_fetch(specs, sem, layer, pt_ref, nch, pages, lambda cc: nch - 1 - cc)
    nr = N_HEADS * dec

    qf = _stack_slabs(qf_ref[...], N_HEADS)[:, :HEAD_DIM]
    qm = _stack_slabs(qm_ref[...], N_HEADS, 2)
    qlat, qrope = qm[:, :LANES], qm[:, LANES:LANES + MLA_ROPE]
    qd_all = qd_ref[...]
    qd = jnp.concatenate([qd_all[:, (2 * h + g) * LANES:(2 * h + g + 1) * LANES]
                          for g in range(2) for h in range(N_HEADS)], axis=0)[:, :HEAD_DIM]
    qs = _stack_slabs(qs_ref[...], N_HEADS)[:, :HEAD_DIM]
    ecol = ecol_ref[...]
    thr = thr_ref[:, 0:1]

    def process(fkv, after, ckv, krt, dkv, skv, keys, t5d, t5s, is_new):
        nk = after.shape[1]
        if is_new:
            trow = lax.broadcasted_iota(I32, (dec, nk), 0)
            scol = lax.broadcasted_iota(I32, (dec, nk), 1)
            ok8 = jnp.where(scol <= trow, 1.0, 0.0)
            ok = jnp.concatenate([ok8] * N_HEADS, axis=0) > 0.5
            ok2 = jnp.concatenate([ok8] * (2 * N_HEADS), axis=0) > 0.5
        s = _bdot(qf, fkv[0]) * HEAD_DIM ** -0.5
        excl = after + rcar[...]
        bias = jnp.concatenate([jnp.broadcast_to(excl[h:h + 1, :], (dec, nk)) for h in range(N_HEADS)], axis=0)
        s = s + bias - ecol
        if is_new:
            s = jnp.where(ok, s, NEG_INF)
        _online_update_fn(s, lambda p: _bdot_nt(p, fkv[1]), mf, lf_, af)
        s = (_bdot_nt(qlat, ckv) + _bdot(qrope, krt)) * (MLA_NOPE + MLA_ROPE) ** -0.5
        if is_new:
            s = jnp.where(ok, s, NEG_INF)
        _online_update_fn(s, lambda p: _bdot(p, ckv), mm, lm, am)
        s = _bdot(qd, dkv[0]) * DIFF_HALF ** -0.5 + t5d
        if is_new:
            s = jnp.where(ok2, s, NEG_INF)
        _online_update_fn(s, lambda p: _bdot_nt(p, dkv[1]), md, ld, ad)
        s = _bdot(qs, skv[0]) * HEAD_DIM ** -0.5 + t5s
        sel8 = jnp.where(keys >= thr, 1.0, 0.0)
        if is_new:
            sel8 = sel8 * ok8
        keep = jnp.concatenate([sel8] * N_HEADS, axis=0) > 0.5
        s = jnp.where(keep, s, NEG_INF)
        _online_update_fn(s, lambda p: _bdot_nt(p, skv[1]), ms, ls, as_, keep=keep)

    @pl.when(c == 0)
    def _():
        for m_, l_, a_ in ((mf, lf_, af), (mm, lm, am), (md, ld, ad), (ms, ls, as_)):
            _init_state(m_, l_, a_)
        rcar[...] = jnp.zeros_like(rcar)
        lgf_new = nl_ref[...]
        incl_new = _suffix_sum(lgf_new)
        process(nf_ref[...], incl_new - lgf_new, nc_ref[...], nk_ref[...], nd_ref[...], ns_ref[...], kn_ref[...],
                tnd_ref[...], tns_ref[...], True)
        rcar[...] = incl_new[:, 0:1]

    process(fbuf[slot], fb_ref[...], cbuf[slot], kbuf[slot], dbuf[slot], sbuf[slot], kp_ref[...],
            tbd_ref[...], tbs_ref[...], False)

    @pl.when(c == nch - 1)
    def _():
        def put(o_ref, o):
            w = o.shape[1]
            for h in range(N_HEADS):
                if w < LANES:
                    o_ref[h] = jnp.zeros((dec, LANES), BF16)
                    o_ref[h, :, 0:w] = o[h * dec:(h + 1) * dec].astype(BF16)
                else:
                    o_ref[h] = o[h * dec:(h + 1) * dec].astype(BF16)

        put(of_ref, af[...] / lf_[...])
        put(om_ref, am[...] / lm[...])
        od = ad[...] / ld[...]
        od = od[:nr] - lam_ref[0] * od[nr:]
        y = od * lax.rsqrt(jnp.mean(od * od, axis=-1, keepdims=True) + EPS) * g_ref[...] * (1.0 - lam_init)
        put(od_ref, y)
        put(os_ref, as_[...] / ls[...])


def _split_rows(rows, lead):
    s = lambda i, n: rows[:, i * LANES:i * LANES + n]
    return {
        'fox_kv': s(0, 2 * HEAD_DIM).reshape(lead + (2, HEAD_DIM)),
        'fox_logf': s(1, N_HEADS).reshape(lead + (N_HEADS,)),
        'mla_ckv': s(2, MLA_KV_RANK).reshape(lead + (MLA_KV_RANK,)),
        'mla_krope': s(3, MLA_ROPE).reshape(lead + (MLA_ROPE,)),
        'diff_kv': s(4, 2 * HEAD_DIM).reshape(lead + (2, HEAD_DIM)),
        'dsa_kv': s(5, 2 * HEAD_DIM).reshape(lead + (2, HEAD_DIM)),
        'dsa_kidx': s(6, IDX_DIM).reshape(lead + (IDX_DIM,)),
    }


def _ffn_and_norm(l, routed, gate2, ln_g2, ln_b2, ew, rows_per_mod, alpha):
    x1, h2, ti, tg, rk, cnt = routed
    n_tok = x1.shape[0]
    counts = cnt[0, :N_EXPERTS].astype(I32)
    dest, src, blk_expert, n_blocks = _routing_tables(ti[:, :TOP_K], rk[:, :TOP_K], counts, n_tok)
    y_pad = _moe_blocks(l, h2, blk_expert, src, n_blocks, *ew)
    return _combine_final(x1, y_pad, dest, tg, gate2, ln_g2, ln_b2, rows_per_mod, alpha)


def _lambda(l, diff_lambda):
    lam_init = 0.8 - 0.6 * math.exp(-0.3 * l)
    dl = diff_lambda[l]
    lam = (jnp.exp(jnp.sum(dl[0] * dl[1]).astype(F32)) - jnp.exp(jnp.sum(dl[2] * dl[3]).astype(F32)) + lam_init)
    return lam, lam_init


def _decode_attention(l, pr, caches_t, page_table, t5_table, lam, lam_init, subln_lo, bd, dec):
    rows, cum, qf, qm, qd, qs, qi, iw = pr
    fox_t, logf_t, ckv_c, kr_t, diff_t, dsa_t, kidx_t = caches_t
    n_pages = page_table.shape[1]
    pages = min(PAGES_PER_STEP, n_pages)
    assert n_pages % pages == 0
    nch = n_pages // pages
    gk = pages * LANES
    nk_past = n_pages * LANES
    nsel = min(IDX_TOPK, (nk_past + dec) // 4)

    rows3 = rows.reshape(bd, dec, rows.shape[-1])

    def key_minor(slab, n):
        x = jnp.transpose(rows3[:, :, slab * LANES:slab * LANES + n], (0, 2, 1))
        return _pad_last(x, LANES)

    nf = key_minor(0, 2 * HEAD_DIM).reshape(bd, 2, HEAD_DIM, LANES)
    nl = key_minor(1, N_HEADS)
    nc = jnp.pad(rows3[:, :, 2 * LANES:3 * LANES], ((0, 0), (0, LANES - dec), (0, 0)))
    nkr = key_minor(3, MLA_ROPE)
    nd = key_minor(4, 2 * HEAD_DIM).reshape(bd, 2, HEAD_DIM, LANES)
    ns = key_minor(5, 2 * HEAD_DIM).reshape(bd, 2, HEAD_DIM, LANES)
    nidx = key_minor(6, IDX_DIM)
    cum3 = cum.reshape(bd, dec, LANES)[:, :, :N_HEADS]
    ecol = jnp.transpose(cum3[:, dec - 1:dec, :] - cum3, (0, 2, 1)).reshape(bd, N_HEADS * dec, 1)

    sem = ("arbitrary", "arbitrary")
    row_blk = lambda n: pl.BlockSpec((dec, n), lambda b, c, *_: (b, 0))
    per_b = lambda a: pl.BlockSpec((None,) + a.shape[1:], lambda b, c, *_: (b,) + (0,) * (a.ndim - 1))
    anyspec = pl.BlockSpec(memory_space=pl.ANY)

    gs_idx = pltpu.PrefetchScalarGridSpec(
        num_scalar_prefetch=1, grid=(bd, 1),
        in_specs=[anyspec, anyspec, row_blk(8 * LANES), row_blk(LANES), per_b(nidx)],
        out_specs=[pl.BlockSpec((None, dec, nk_past), lambda b, c, pt: (b, 0, 0)),
                   pl.BlockSpec((None, dec, LANES), lambda b, c, pt: (b, 0, 0)),
                   pl.BlockSpec((None, dec, LANES), lambda b, c, pt: (b, 0, 0)),
                   pl.BlockSpec((None, N_HEADS, n_pages, LANES), lambda b, c, pt: (b, 0, 0, 0))],
        scratch_shapes=[pltpu.VMEM((2, IDX_DIM, nk_past), F32), pltpu.VMEM((2, N_HEADS, n_pages, LANES), F32),
                        pltpu.SemaphoreType.DMA((2, 2)),
                        pltpu.VMEM((1, dec, nk_past), I32), pltpu.VMEM((dec, LANES), I32)])
    keys_past, keys_new, thr, after = pl.pallas_call(
        functools.partial(_dec_idx_kernel, layer=l, nch=1, pages=n_pages, dec=dec, nsel=nsel),
        grid_spec=gs_idx,
        out_shape=[jax.ShapeDtypeStruct((bd, dec, nk_past), I32), jax.ShapeDtypeStruct((bd, dec, LANES), I32),
                   jax.ShapeDtypeStruct((bd, dec, LANES), I32),
                   jax.ShapeDtypeStruct((bd, N_HEADS, n_pages, LANES), F32)],
        compiler_params=_cparams(sem), name="decode_indexer",
    )(page_table, kidx_t, logf_t, qi, iw, nidx)
    after = after.reshape(bd, N_HEADS, nk_past)

    tpos = jnp.arange(dec, dtype=I32)[:, None]
    rel_last = gk + tpos - jnp.arange(gk, dtype=I32)[None, :]
    rel_far = jnp.full((dec, gk), 2 * MAX_DISTANCE, I32)
    rel_new = tpos - jnp.arange(LANES, dtype=I32)[None, :]

    def tiles(cols, reps):
        tb = jnp.stack([_t5_rows(cols, rel_far, reps), _t5_rows(cols, rel_last, reps)], axis=0).astype(F32)
        return tb, _t5_rows(cols, rel_new, reps).astype(F32)

    tbd, tnd = tiles(t5_table[:, :N_HEADS], 2)
    tbs, tns = tiles(t5_table[:, N_HEADS:], 1)
    recent = lambda a: pl.BlockSpec((None,) + a.shape[1:], lambda b, c, *_: (jnp.where(c == 0, 1, 0), 0, 0))
    full = lambda a: pl.BlockSpec(a.shape, lambda b, c, *_: (0,) * a.ndim)
    nr = N_HEADS * dec
    col = lambda n: pltpu.VMEM((n, 1), F32)
    ospec = pl.BlockSpec((None, N_HEADS, dec, LANES), lambda b, c, *_: (b, 0, 0, 0))
    oshape = jax.ShapeDtypeStruct((bd, N_HEADS, dec, LANES), BF16)
    gs_att = pltpu.PrefetchScalarGridSpec(
        num_scalar_prefetch=1, grid=(bd, nch),
        in_specs=[pl.BlockSpec(memory_space=pltpu.SMEM)] + [anyspec] * 5 + [
            row_blk(4 * LANES), row_blk(8 * LANES), row_blk(8 * LANES), row_blk(4 * LANES), per_b(ecol),
            pl.BlockSpec((None, N_HEADS, gk), lambda b, c, pt: (b, 0, nch - 1 - c)),
            pl.BlockSpec((None, dec, gk), lambda b, c, pt: (b, 0, nch - 1 - c)), per_b(keys_new), per_b(thr),
            per_b(nf), per_b(nl), per_b(nc), per_b(nkr), per_b(nd), per_b(ns),
            recent(tbd), recent(tbs), full(tnd), full(tns), full(subln_lo)],
        out_specs=[ospec] * 4,
        scratch_shapes=[pltpu.VMEM((2, 2, HEAD_DIM, gk), F32),
                        pltpu.VMEM((2, gk, LANES), F32), pltpu.VMEM((2, MLA_ROPE, gk), F32),
                        pltpu.VMEM((2, 2, HEAD_DIM, gk), F32), pltpu.VMEM((2, 2, HEAD_DIM, gk), F32),
                        pltpu.SemaphoreType.DMA((2, 5)),
                        col(nr), col(nr), pltpu.VMEM((nr, HEAD_DIM), F32),
                        col(nr), col(nr), pltpu.VMEM((nr, LANES), F32),
                        col(2 * nr), col(2 * nr), pltpu.VMEM((2 * nr, HEAD_DIM), F32),
                        col(nr), col(nr), pltpu.VMEM((nr, HEAD_DIM), F32),
                        pltpu.VMEM((N_HEADS, 1), F32)])
    return pl.pallas_call(
        functools.partial(_dec_att_kernel, layer=l, nch=nch, pages=pages, dec=dec, lam_init=lam_init),
        grid_spec=gs_att,
        out_shape=[oshape] * 4,
        compiler_params=_cparams(sem), name="decode_attention",
    )(page_table, lam.reshape(1), fox_t, ckv_c, kr_t, diff_t, dsa_t,
      qf, qm, qd, qs, ecol, after, keys_past, keys_new, thr, nf, nl, nc, nkr, nd, ns, tbd, tbs, tnd, tns, subln_lo)


def _sample_layer(l, x, mod, wts, w_out_l, ln_g, ln_b, ew, t5_table, diff_lambda, diff_subln, alpha,
                  caches_t, page_table):
    bd, dec, d = x.shape
    past_len = page_table.shape[1] * LANES
    mod_rows = jnp.repeat(mod, dec, axis=0)
    m = [mod_rows[:, k * d:(k + 1) * d] for k in range(6)]
    cos_t, sin_t = _rope_tables(past_len + jnp.arange(dec, dtype=I32))
    cos_t, sin_t = jnp.tile(cos_t, (bd, 1)), jnp.tile(sin_t, (bd, 1))
    x2d = x.reshape(bd * dec, d)
    pr = _project(x2d, m[1], m[0], wts, cos_t, sin_t, dec, 1, False)
    lam, lam_init = _lambda(l, diff_lambda)
    o_list = _decode_attention(l, pr, caches_t, page_table, t5_table, lam, lam_init, diff_subln[l][None, :], bd, dec)
    wo = _prep_w_out(w_out_l, False)
    routed = _out_router(x2d, o_list, wo, wts['wv'], m[2], m[4], m[3], ln_g[l, 0:1], ln_b[l, 0:1],
                         wts['rw'], wts['rb'], 1, alpha)
    x2 = _ffn_and_norm(l, routed, m[5], ln_g[l, 1:2], ln_b[l, 1:2], ew, 1, alpha)
    return x2.reshape(bd, dec, d), _split_rows(pr[0], (bd, dec))


def _prompt_layer(l, x, mod, wts, w_out_l, ln_g, ln_b, ew, t5_table, diff_lambda, alpha):
    bsz, seq, d = x.shape
    m = [mod[:, None, k * d:(k + 1) * d] for k in range(6)]
    cos_t, sin_t = _rope_tables(jnp.arange(seq, dtype=I32))
    cos_t, sin_t = jnp.tile(cos_t, (bsz, 1)), jnp.tile(sin_t, (bsz, 1))
    x2d = x.reshape(bsz * seq, d)
    pr = _project(x2d, m[1], m[0], wts, cos_t, sin_t, seq, seq, True)
    lam, lam_init = _lambda(l, diff_lambda)
    o_list = _prompt_attention(pr, t5_table, lam, lam_init, wts['subln'], bsz, seq)
    wo = _prep_w_out(w_out_l, True)
    routed = _out_router(x2d, o_list, wo, wts['wv'], m[2], m[4], m[3], ln_g[l, 0:1], ln_b[l, 0:1],
                         wts['rw'], wts['rb'], seq, alpha)
    x2 = _ffn_and_norm(l, routed, m[5], ln_g[l, 1:2], ln_b[l, 1:2], ew, seq, alpha)
    return x2.reshape(bsz, seq, d), _split_rows(pr[0], (bsz, seq))


def kernel(x_prompt, x_sample, c_prompt, c_sample, cache_fox_kv, cache_fox_logf, cache_mla_ckv, cache_mla_krope,
           cache_diff_kv, cache_dsa_kv, cache_dsa_kidx, page_table, w_in, fox_f_bias, mla_q_norm, mla_w_uq,
           mla_kv_norm, mla_w_uk, mla_w_uv, diff_lambda, diff_subln, w_out, w_ada, b_ada, ln_g, ln_b,
           router_w, router_b, exp_w_gu, exp_b_gu, exp_w_dn, exp_b_dn, t5_table):
    depth = w_in.shape[0]
    alpha = (2 * depth) ** 0.25
    bsz, bd = x_prompt.shape[0], x_sample.shape[0]
    caches_t = (jnp.transpose(cache_fox_kv, (0, 1, 3, 4, 2)), jnp.transpose(cache_fox_logf, (0, 1, 3, 2)),
                cache_mla_ckv, jnp.transpose(cache_mla_krope, (0, 1, 3, 2)),
                jnp.transpose(cache_diff_kv, (0, 1, 3, 4, 2)), jnp.transpose(cache_dsa_kv, (0, 1, 3, 4, 2)),
                jnp.transpose(cache_dsa_kidx, (0, 1, 3, 2)))
    n_c = bsz + bd
    c_all = jnp.pad(jnp.concatenate([c_prompt, c_sample], axis=0), ((0, -n_c % 8), (0, 0)))
    mod_all = _ada(c_all, w_ada, b_ada)
    ew = (exp_w_gu, exp_b_gu, exp_w_dn, exp_b_dn)
    xp, xs = x_prompt, x_sample
    rows_p, rows_s = [], []
    for l in range(depth):
        wts = _prep_layer_weights(l, w_in, fox_f_bias, mla_q_norm, mla_w_uq, mla_kv_norm, mla_w_uk, mla_w_uv,
                                  diff_subln, w_out, router_w, router_b)
        xp, rp = _prompt_layer(l, xp, mod_all[l, :bsz], wts, w_out[l], ln_g, ln_b, ew, t5_table, diff_lambda, alpha)
        xs, rs = _sample_layer(l, xs, mod_all[l, bsz:n_c], wts, w_out[l], ln_g, ln_b, ew, t5_table, diff_lambda,
                               diff_subln, alpha, caches_t, page_table)
        rows_p.append(rp)
        rows_s.append(rs)

    def stack(rows, name):
        return jnp.stack([r[name] for r in rows], axis=0)

    out = [xp, xs]
    for name in ('fox_kv', 'fox_logf', 'mla_ckv', 'mla_krope', 'diff_kv', 'dsa_kv', 'dsa_kidx'):
        out += [stack(rows_p, name), stack(rows_s, name)]
    return tuple(out)
```
